```python
import functools
import jax, jax.numpy as jnp
from jax import lax
import numpy as np

D_MODEL = 2048
BATCH = 1
SEQ = 16384
DEPTH = 1
DEC_BATCH = 32
DEC_SEQ = 4
PAST_LEN = 16384
PAGE_SIZE = 128

ATT_HEADS = 16
HEAD_DIM = 64
ATT_WIDTH = ATT_HEADS * HEAD_DIM
CONV_WIDTH = D_MODEL - ATT_WIDTH
CONV_GROUPS = 16
CONV_K = 3
MIX_WIDTH = ATT_WIDTH + CONV_WIDTH
IN_WIDTH = 3 * ATT_WIDTH + 3 * CONV_WIDTH
SPLITS = (ATT_WIDTH, 2 * ATT_WIDTH, 3 * ATT_WIDTH,
          3 * ATT_WIDTH + CONV_WIDTH, 3 * ATT_WIDTH + 2 * CONV_WIDTH)
Q_BLOCK = 128
SB_SCALE = HEAD_DIM ** -0.5
SB_BIAS_INIT = -7.0
PEER_HEADS = 8
N_KEYS = 128
N_EXPERTS = N_KEYS * N_KEYS
PEER_QDIM = 256
PEER_HALF = PEER_QDIM // 2
PEER_TOPK = 16
PEER_TOK_BLOCK = 128
N_MOD = 6
EPS = 1e-6

kernel_name = 'stick_conv_peer_hybrid_step'


def rmsnorm(x, g):
    xf = x.astype(jnp.float32)
    r = lax.rsqrt(jnp.mean(xf * xf, axis=-1, keepdims=True) + EPS)
    return (xf * r).astype(x.dtype) * g


def group_norm(x, g, n_groups):
    b, t, w = x.shape
    gs = w // n_groups
    return rmsnorm(x.reshape(b, t, n_groups, gs), g.reshape(n_groups, gs)).reshape(b, t, w)


def stick_breaking(q, keys, values, q_pos, k_pos, bias):
    qf = q.astype(jnp.float32) * SB_SCALE
    z = jnp.concatenate([jnp.einsum('bqhd,bkhd->bhqk', qf, k.astype(jnp.float32)) for k in keys],
                        axis=-1)
    z = z + bias.astype(jnp.float32)[None, :, None, None]
    mask = k_pos[None, :] < q_pos[:, None]
    log1mb = jnp.where(mask, jax.nn.log_sigmoid(-z), 0.0)
    suffix = lax.cumsum(log1mb, axis=3, reverse=True) - log1mb
    a = jnp.where(mask, jnp.exp(jax.nn.log_sigmoid(z) + suffix), 0.0)
    out = None
    off = 0
    for v in values:
        n = v.shape[1]
        part = jnp.einsum('bhqk,bkhd->bqhd', a[..., off:off + n], v.astype(jnp.float32))
        out = part if out is None else out + part
        off += n
    return out.astype(q.dtype)


def attend_prompt(q, k, v, bias):
    b, t = q.shape[:2]
    nb = t // Q_BLOCK
    k_pos = jnp.arange(t)
    qb = jnp.moveaxis(q.reshape(b, nb, Q_BLOCK, ATT_HEADS, HEAD_DIM), 1, 0)

    def block(args):
        qi, start = args
        return stick_breaking(qi, [k], [v], start + jnp.arange(Q_BLOCK), k_pos, bias)

    out = lax.map(block, (qb, jnp.arange(nb) * Q_BLOCK))
    return jnp.moveaxis(out, 0, 1).reshape(b, t, ATT_HEADS, HEAD_DIM)


def attend_paged(q, k, v, bias, cache_k, cache_v, page_table):
    b, t = q.shape[:2]
    past = page_table.shape[1] * PAGE_SIZE
    k_past = cache_k[page_table].reshape(b, past, ATT_HEADS, HEAD_DIM)
    v_past = cache_v[page_table].reshape(b, past, ATT_HEADS, HEAD_DIM)
    k_pos = jnp.arange(past + t)
    q_pos = past + jnp.arange(t)
    return stick_breaking(q, [k_past, k], [v_past, v], q_pos, k_pos, bias)


def peer(h, w_query, sub_keys, expert_u, expert_v):
    n = h.shape[0]
    pad = (-n) % PEER_TOK_BLOCK
    blocks = jnp.pad(h, ((0, pad), (0, 0))).reshape(-1, PEER_TOK_BLOCK, D_MODEL)

    def block(hb):
        q = (hb @ w_query).reshape(PEER_TOK_BLOCK, PEER_HEADS, 2, PEER_HALF)
        s = jnp.einsum('nhpd,hpkd->nhpk', q.astype(jnp.float32), sub_keys.astype(jnp.float32))
        sv, si = lax.top_k(s, PEER_TOPK)
        cand = (sv[:, :, 0, :, None] + sv[:, :, 1, None, :]).reshape(PEER_TOK_BLOCK, PEER_HEADS, -1)
        cid = (si[:, :, 0, :, None] * N_KEYS + si[:, :, 1, None, :]).reshape(PEER_TOK_BLOCK, PEER_HEADS, -1)
        top_v, top_p = lax.top_k(cand, PEER_TOPK)
        eid = jnp.take_along_axis(cid, top_p, axis=-1)
        g = jax.nn.softmax(top_v, axis=-1)
        act = jax.nn.gelu(jnp.einsum('nd,nhkd->nhk', hb, expert_u[eid]).astype(jnp.float32))
        wgt = (g * act).astype(hb.dtype)
        return jnp.einsum('nhk,nhkd->nd', wgt, expert_v[eid])

    return lax.map(block, blocks).reshape(-1, D_MODEL)[:n]


def trunk_layer(x, c, conv_buf, attend, w_ada, b_ada, norm1, w_in, sb_bias, conv_w, gn_attn, gn_conv,
                w_o, norm2, w_query, sub_keys, expert_u, expert_v):
    b, t, _ = x.shape
    mod = jax.nn.silu(c) @ w_ada + b_ada
    sh1, sc1, g1, sh2, sc2, g2 = [m[:, None, :] for m in jnp.split(mod, N_MOD, axis=-1)]
    h = rmsnorm(x, norm1) * (1 + sc1) + sh1
    proj = h @ w_in
    q, k, v, gb, gc, hc = jnp.split(proj, SPLITS, axis=-1)
    q = q.reshape(b, t, ATT_HEADS, HEAD_DIM)
    k = k.reshape(b, t, ATT_HEADS, HEAD_DIM)
    v = v.reshape(b, t, ATT_HEADS, HEAD_DIM)
    attn = attend(q, k, v, sb_bias).reshape(b, t, ATT_WIDTH)
    u = gc * hc
    up = jnp.concatenate([conv_buf.astype(u.dtype), u], axis=1)
    conv = conv_w[0] * up[:, 0:t]
    for j in range(1, CONV_K):
        conv = conv + conv_w[j] * up[:, j:j + t]
    y_conv = gb * conv
    merged = jnp.concatenate([group_norm(attn, gn_attn, ATT_HEADS),
                              group_norm(y_conv, gn_conv, CONV_GROUPS)], axis=-1) @ w_o
    x = x + g1 * merged
    h2 = rmsnorm(x, norm2) * (1 + sc2) + sh2
    ffn = peer(h2.reshape(b * t, D_MODEL), w_query, sub_keys, expert_u, expert_v).reshape(b, t, D_MODEL)
    x = x + g2 * ffn
    return x, k, v, up[:, t:]


def setup_inputs(seed: int = 0) -> dict:
    key = jax.random.key(seed)
    ks = jax.random.split(key, 24)
    f32 = jnp.float32
    n_pages = PAST_LEN // PAGE_SIZE
    n_used = DEC_BATCH * n_pages
    n_phys = n_used + max(n_used // 4, 1)

    def nrm(k, shape, s):
        return jax.random.normal(k, shape, f32) * s

    page_table = jax.random.permutation(ks[5], n_phys)[:n_used].reshape(DEC_BATCH, n_pages).astype(jnp.int32)
    return {
        'x_prompt': nrm(ks[0], (BATCH, SEQ, D_MODEL), 1.0),
        'x_sample': nrm(ks[1], (DEC_BATCH, DEC_SEQ, D_MODEL), 1.0),
        'cache_k': nrm(ks[2], (DEPTH, n_phys, PAGE_SIZE, ATT_HEADS, HEAD_DIM), 1.0),
        'cache_v': nrm(ks[3], (DEPTH, n_phys, PAGE_SIZE, ATT_HEADS, HEAD_DIM), 1.0),
        'state_conv': nrm(ks[4], (DEPTH, DEC_BATCH, CONV_K - 1, CONV_WIDTH), 1.0),
        'page_table': page_table,
        'c_prompt': nrm(ks[6], (BATCH, D_MODEL), 1.0),
        'c_sample': nrm(ks[7], (DEC_BATCH, D_MODEL), 1.0),
        'w_ada': nrm(ks[8], (DEPTH, D_MODEL, N_MOD * D_MODEL), 0.5 * D_MODEL ** -0.5),
        'b_ada': nrm(ks[9], (DEPTH, N_MOD * D_MODEL), 0.01),
        'norm1': 1.0 + nrm(ks[10], (DEPTH, D_MODEL), 0.02),
        'w_in': nrm(ks[11], (DEPTH, D_MODEL, IN_WIDTH), D_MODEL ** -0.5),
        'sb_bias': SB_BIAS_INIT + nrm(ks[22], (DEPTH, ATT_HEADS), 0.5),
        'conv_w': nrm(ks[12], (DEPTH, CONV_K, CONV_WIDTH), CONV_K ** -0.5),
        'gn_attn': 1.0 + nrm(ks[13], (DEPTH, ATT_WIDTH), 0.02),
        'gn_conv': 1.0 + nrm(ks[14], (DEPTH, CONV_WIDTH), 0.02),
        'w_o': nrm(ks[15], (DEPTH, MIX_WIDTH, D_MODEL), MIX_WIDTH ** -0.5),
        'norm2': 1.0 + nrm(ks[16], (DEPTH, D_MODEL), 0.02),
        'w_query': nrm(ks[17], (DEPTH, D_MODEL, PEER_HEADS * PEER_QDIM), D_MODEL ** -0.5),
        'sub_keys': nrm(ks[18], (DEPTH, PEER_HEADS, 2, N_KEYS, PEER_HALF), PEER_HALF ** -0.5),
        'expert_u': nrm(ks[19], (DEPTH, N_EXPERTS, D_MODEL), D_MODEL ** -0.5),
        'expert_v': nrm(ks[20], (DEPTH, N_EXPERTS, D_MODEL), 1.0),
        'norm_f': 1.0 + nrm(ks[21], (D_MODEL,), 0.02),
    }


def reference(x_prompt, x_sample, cache_k, cache_v, state_conv, page_table, c_prompt, c_sample,
              w_ada, b_ada, norm1, w_in, sb_bias, conv_w, gn_attn, gn_conv, w_o, norm2, w_query,
              sub_keys, expert_u, expert_v, norm_f):
    xp, xs = x_prompt, x_sample
    kp_l, vp_l, cp_l, ks_l, vs_l, cs_l = [], [], [], [], [], []
    for l in range(DEPTH):
        wl = (w_ada[l], b_ada[l], norm1[l], w_in[l], sb_bias[l], conv_w[l], gn_attn[l], gn_conv[l],
              w_o[l], norm2[l], w_query[l], sub_keys[l], expert_u[l], expert_v[l])
        zero_buf = jnp.zeros((xp.shape[0], CONV_K - 1, CONV_WIDTH), xp.dtype)
        xp, kp, vp, cp = trunk_layer(xp, c_prompt, zero_buf, attend_prompt, *wl)
        attend_s = functools.partial(attend_paged, cache_k=cache_k[l], cache_v=cache_v[l],
                                     page_table=page_table)
        xs, ks_, vs_, cs_ = trunk_layer(xs, c_sample, state_conv[l], attend_s, *wl)
        kp_l.append(kp); vp_l.append(vp); cp_l.append(cp)
        ks_l.append(ks_); vs_l.append(vs_); cs_l.append(cs_)
    y_prompt = rmsnorm(xp, norm_f)
    y_sample = rmsnorm(xs, norm_f)
    return (y_prompt, y_sample, jnp.stack(kp_l), jnp.stack(vp_l), jnp.stack(cp_l),
            jnp.stack(ks_l), jnp.stack(vs_l), jnp.stack(cs_l))
```

```python
import functools

import jax
import jax.numpy as jnp
from jax import lax
from jax.experimental import pallas as pl
from jax.experimental.pallas import tpu as pltpu

F32 = jnp.float32
BF16 = jnp.bfloat16
EPS = 1e-6

ATT_HEADS = 16
HEAD_DIM = 64
ATT_WIDTH = ATT_HEADS * HEAD_DIM
SB_SCALE = HEAD_DIM ** -0.5
PEER_HEADS = 8
N_KEYS = 128
PEER_TOPK = 16
NOT_RANKED = 99.0
NEG_INF = float("-inf")

VMEM_LIMIT_BYTES = 56 * 1024 * 1024
LANES = 128

ATT_TILE = 256
PROJ_ROWS = 256
POST_ROWS = 256
ROUTE_ROWS = 512
DENSE_ROWS = 512
DENSE_EXPERTS = 512


def _cparams(n_axes):
    return pltpu.CompilerParams(dimension_semantics=("arbitrary",) * n_axes,
                                vmem_limit_bytes=VMEM_LIMIT_BYTES)


def _nt_dot(a, b):
    return lax.dot_general(a, b, (((1,), (1,)), ((), ())), preferred_element_type=F32)


def _split_bf16(x):
    hi = x.astype(BF16)
    lo = (x - hi.astype(F32)).astype(BF16)
    return hi, lo


def _softplus(z):
    return jnp.maximum(z, 0.0) + jnp.log(1.0 + jnp.exp(-jnp.abs(z)))


def _mod_kernel(c_ref, w_ref, b_ref, o_ref):
    c = c_ref[...]
    s = c * (1.0 / (1.0 + jnp.exp(-c)))
    s_hi, s_lo = _split_bf16(s)
    w = w_ref[...]
    w_hi, w_lo = _split_bf16(w)
    acc = jnp.dot(s_hi, w_hi, preferred_element_type=F32)
    acc += jnp.dot(s_hi, w_lo, preferred_element_type=F32)
    acc += jnp.dot(s_lo, w_hi, preferred_element_type=F32)
    o_ref[...] = acc + b_ref[...]


def _mod_call(c, w_ada, b_ada):
    m, d = c.shape
    n = w_ada.shape[1]
    tn = 512
    return pl.pallas_call(
        _mod_kernel,
        grid=(n // tn,),
        in_specs=[pl.BlockSpec((m, d), lambda j: (0, 0)),
                  pl.BlockSpec((d, tn), lambda j: (0, j)),
                  pl.BlockSpec((1, tn), lambda j: (0, j))],
        out_specs=pl.BlockSpec((m, tn), lambda j: (0, j)),
        out_shape=jax.ShapeDtypeStruct((m, n), F32),
        compiler_params=_cparams(1),
    )(c, w_ada, b_ada.reshape(1, n))


def _proj_kernel(x_ref, sh_ref, sc_ref, g_ref, w_ref,
                 q_ref, k_ref, kb_ref, v_ref, vt_ref, gb_ref, u_ref, h_s, gc_s, *, tk):
    j = pl.program_id(1)

    @pl.when(j == 0)
    def _():
        x = x_ref[...]
        r = lax.rsqrt(jnp.mean(x * x, axis=-1, keepdims=True) + EPS)
        h = (x * r) * g_ref[...] * (1.0 + sc_ref[...]) + sh_ref[...]
        h_s[...] = h.astype(BF16)

    p = jnp.dot(h_s[...], w_ref[...], preferred_element_type=F32)

    @pl.when(j == 0)
    def _():
        q_ref[...] = (p * SB_SCALE).astype(BF16)

    @pl.when(j == 1)
    def _():
        k_ref[...] = p
        kb_ref[...] = p.astype(BF16)

    @pl.when(j == 2)
    def _():
        v_ref[...] = p
        for c in range(p.shape[0] // tk):
            vt_ref[c] = p[c * tk:(c + 1) * tk, :].T.astype(BF16)

    @pl.when(j == 3)
    def _():
        gb_ref[...] = p

    @pl.when(j == 4)
    def _():
        gc_s[...] = p

    @pl.when(j == 5)
    def _():
        u_ref[...] = gc_s[...] * p


def _proj_call(x, sh, sc, norm_g, w_in_b, tm, tk):
    n, d = x.shape
    w = ATT_WIDTH
    per_row = sh.shape[0] != 1
    mod_spec = (pl.BlockSpec((tm, d), lambda i, j: (i, 0)) if per_row
                else pl.BlockSpec((1, d), lambda i, j: (0, 0)))
    row = lambda i, j: (i, 0)
    outs = (
        jax.ShapeDtypeStruct((n, w), BF16),
        jax.ShapeDtypeStruct((n, w), F32),
        jax.ShapeDtypeStruct((n, w), BF16),
        jax.ShapeDtypeStruct((n, w), F32),
        jax.ShapeDtypeStruct((n // tk, w, tk), BF16),
        jax.ShapeDtypeStruct((n, w), F32),
        jax.ShapeDtypeStruct((n, w), F32),
    )
    return pl.pallas_call(
        functools.partial(_proj_kernel, tk=tk),
        grid=(n // tm, 6),
        in_specs=[pl.BlockSpec((tm, d), row), mod_spec, mod_spec,
                  pl.BlockSpec((1, d), lambda i, j: (0, 0)),
                  pl.BlockSpec((d, w), lambda i, j: (0, j))],
        out_specs=(pl.BlockSpec((tm, w), row), pl.BlockSpec((tm, w), row), pl.BlockSpec((tm, w), row),
                   pl.BlockSpec((tm, w), row), pl.BlockSpec((tm // tk, w, tk), lambda i, j: (i, 0, 0)),
                   pl.BlockSpec((tm, w), row), pl.BlockSpec((tm, w), row)),
        out_shape=outs,
        scratch_shapes=[pltpu.VMEM((tm, d), BF16), pltpu.VMEM((tm, w), F32)],
        compiler_params=_cparams(2),
    )(x, sh, sc, norm_g, w_in_b)


def _attn_prompt_kernel(bias_ref, q_ref, kb_ref, vt_ref, o_ref, acc_s, *, t):
    hp = pl.program_id(0)
    i = pl.program_id(1)
    q2 = q_ref[...]
    lane = lax.broadcasted_iota(jnp.int32, q2.shape, 1)
    zero = jnp.zeros_like(q2)
    q_heads = (jnp.where(lane < HEAD_DIM, q2, zero), jnp.where(lane >= HEAD_DIM, q2, zero))
    biases = (bias_ref[2 * hp], bias_ref[2 * hp + 1])
    r_io = lax.broadcasted_iota(jnp.int32, (t, t), 0)
    c_io = lax.broadcasted_iota(jnp.int32, (t, t), 1)
    upper = r_io < c_io
    tri = jnp.where(upper, 1.0, 0.0).astype(BF16)
    acc_s[...] = jnp.zeros_like(acc_s)

    def tile(jblk, carry, masked):
        start = pl.multiple_of(jblk * t, t)
        kblk = kb_ref[pl.ds(start, t), :]
        vtb = vt_ref[jblk]
        new = []
        for hh in range(2):
            z = _nt_dot(kblk, q_heads[hh]) + biases[hh]
            sp = _softplus(z)
            spm = jnp.where(upper, sp, 0.0) if masked else sp
            hi, lo = _split_bf16(spm)
            suf = (jnp.dot(tri, hi, preferred_element_type=F32)
                   + jnp.dot(tri, lo, preferred_element_type=F32))
            a = jnp.exp(z - sp - suf - carry[hh])
            if masked:
                a = jnp.where(upper, a, 0.0)
            acc_s[hh] += jnp.dot(vtb, a.astype(BF16), preferred_element_type=F32)
            new.append(carry[hh] + suf[0:1, :] + spm[0:1, :])
        return tuple(new)

    c0 = jnp.zeros((1, t), F32)
    carry = tile(i, (c0, c0), True)
    lax.fori_loop(0, i, lambda s, c: tile(i - 1 - s, c, False), carry)
    row = lax.broadcasted_iota(jnp.int32, (2 * HEAD_DIM, t), 0)
    out_t = jnp.where(row < HEAD_DIM, acc_s[0], acc_s[1])
    o_ref[...] = out_t.T


def _attn_prompt_call(bias, qs, kb, vt, t):
    n, w = qs.shape
    nk = n // t
    return pl.pallas_call(
        functools.partial(_attn_prompt_kernel, t=t),
        grid=(w // LANES, n // t),
        in_specs=[pl.BlockSpec(memory_space=pltpu.SMEM),
                  pl.BlockSpec((t, LANES), lambda hp, i: (i, hp)),
                  pl.BlockSpec((n, LANES), lambda hp, i: (0, hp)),
                  pl.BlockSpec((nk, LANES, t), lambda hp, i: (0, hp, 0))],
        out_specs=pl.BlockSpec((t, LANES), lambda hp, i: (i, hp)),
        out_shape=jax.ShapeDtypeStruct((n, w), F32),
        scratch_shapes=[pltpu.VMEM((2, LANES, t), F32)],
        compiler_params=_cparams(2),
    )(bias, qs, kb, vt)


def _attn_sample_kernel(pt_ref, bias_ref, q_ref, kn_ref, vn_ref, ck_ref, cv_ref, o_ref,
                        qbd_s, kn_s, vn_s, acc_s, carry_s, *, tq, page):
    s = pl.program_id(1)
    n_steps = pl.num_programs(1)
    w = ATT_WIDTH
    r_io = lax.broadcasted_iota(jnp.int32, (page, page), 0)
    c_io = lax.broadcasted_iota(jnp.int32, (page, page), 1)
    tri = jnp.where(r_io < c_io, 1.0, 0.0).astype(BF16)
    head_of_lane = lax.shift_right_logical(lax.broadcasted_iota(jnp.int32, (ATT_HEADS, w), 1), 6)
    head_sel = head_of_lane == lax.broadcasted_iota(jnp.int32, (ATT_HEADS, w), 0)

    def process(kpage, vpage, mask):
        z = _nt_dot(kpage.astype(BF16), qbd_s[...]) + bias_ref[...]
        sp = _softplus(z)
        spm = sp if mask is None else jnp.where(mask, sp, 0.0)
        hi, lo = _split_bf16(spm)
        suf = (jnp.dot(tri, hi, preferred_element_type=F32)
               + jnp.dot(tri, lo, preferred_element_type=F32))
        a = jnp.exp(z - sp - suf - carry_s[...])
        if mask is not None:
            a = jnp.where(mask, a, 0.0)
        acc_s[...] += jnp.dot(a.T.astype(BF16), vpage.astype(BF16), preferred_element_type=F32)
        carry_s[...] += suf[0:1, :] + spm[0:1, :]

    @pl.when(s == 0)
    def _():
        q4 = q_ref[0]
        qbd_s[...] = jnp.zeros_like(qbd_s)
        for qi in range(tq):
            blk = jnp.where(head_sel, jnp.broadcast_to(q4[qi:qi + 1, :], (ATT_HEADS, w)), 0.0)
            qbd_s[qi * ATT_HEADS:(qi + 1) * ATT_HEADS, :] = blk.astype(BF16)
        kn_s[...] = jnp.zeros_like(kn_s)
        vn_s[...] = jnp.zeros_like(vn_s)
        kn_s[0:tq, :] = kn_ref[0]
        vn_s[0:tq, :] = vn_ref[0]
        acc_s[...] = jnp.zeros_like(acc_s)
        carry_s[...] = jnp.zeros_like(carry_s)
        key_row = lax.broadcasted_iota(jnp.int32, (page, LANES), 0)
        q_of_col = lax.shift_right_logical(lax.broadcasted_iota(jnp.int32, (page, LANES), 1), 4)
        process(kn_s[...], vn_s[...], key_row < q_of_col)

    @pl.when(s > 0)
    def _():
        process(ck_ref[0], cv_ref[0], None)

    @pl.when(s == n_steps - 1)
    def _():
        for qi in range(tq):
            blk = acc_s[qi * ATT_HEADS:(qi + 1) * ATT_HEADS, :]
            o_ref[0, qi:qi + 1, :] = jnp.sum(jnp.where(head_sel, blk, 0.0), axis=0, keepdims=True)


def _attn_sample_call(page_table, bias_cols, q, k_new, v_new, cache_k, cache_v):
    b, tq, w = q.shape
    n_pages = page_table.shape[1]
    page = cache_k.shape[1]
    assert tq * ATT_HEADS <= LANES and page == LANES

    def cache_idx(bi, s, pt):
        return (pt[bi * n_pages + jnp.clip(n_pages - s, 0, n_pages - 1)], 0, 0)

    new_spec = pl.BlockSpec((1, tq, w), lambda bi, s, pt: (bi, 0, 0))
    grid_spec = pltpu.PrefetchScalarGridSpec(
        num_scalar_prefetch=1,
        grid=(b, n_pages + 1),
        in_specs=[pl.BlockSpec((1, LANES), lambda bi, s, pt: (0, 0)),
                  new_spec, new_spec, new_spec,
                  pl.BlockSpec((1, page, w), cache_idx),
                  pl.BlockSpec((1, page, w), cache_idx)],
        out_specs=new_spec,
        scratch_shapes=[pltpu.VMEM((LANES, w), BF16), pltpu.VMEM((page, w), F32), pltpu.VMEM((page, w), F32),
                        pltpu.VMEM((LANES, w), F32), pltpu.VMEM((1, LANES), F32)],
    )
    return pl.pallas_call(
        functools.partial(_attn_sample_kernel, tq=tq, page=page),
        grid_spec=grid_spec,
        out_shape=jax.ShapeDtypeStruct((b, tq, w), F32),
        compiler_params=_cparams(2),
    )(page_table.reshape(-1), bias_cols, q, k_new, v_new, cache_k, cache_v)


def _post_kernel(attn_ref, gb_ref, u_ref, p0_ref, p1_ref, x_ref, g1_ref, sc2_ref, sh2_ref,
                 cw_ref, gna_ref, gnc_ref, n2_ref, gm_ref, wo_ref,
                 x1_ref, hhi_ref, hlo_ref, *, group, prev_from_u):
    i = pl.program_id(0)
    u = u_ref[...]
    assert group & (group - 1) == 0
    if prev_from_u:
        first = i == 0
        buf = p0_ref[...]
        prev = p1_ref[...]
        p0 = jnp.where(first, buf[6:7, :], prev[6:7, :])
        p1 = jnp.where(first, buf[7:8, :], prev[7:8, :])
    else:
        p0 = p0_ref[...]
        p1 = p1_ref[...]
    r = lax.broadcasted_iota(jnp.int32, u.shape, 0) & (group - 1)
    um1 = jnp.where(r == 0, p1, pltpu.roll(u, 1, 0))
    um2 = jnp.where(r == 0, p0, jnp.where(r == 1, p1, pltpu.roll(u, 2, 0)))
    cw = cw_ref[...]
    conv = cw[0:1, :] * um2 + cw[1:2, :] * um1 + cw[2:3, :] * u
    yc = gb_ref[...] * conv
    attn = attn_ref[...]
    gm = gm_ref[...]

    def gnorm(v, g):
        msq = jnp.dot((v * v).astype(BF16), gm, preferred_element_type=F32)
        return ((v * lax.rsqrt(msq + EPS)) * g).astype(BF16)

    w = attn.shape[1]
    merged = (jnp.dot(gnorm(attn, gna_ref[...]), wo_ref[0:w, :], preferred_element_type=F32)
              + jnp.dot(gnorm(yc, gnc_ref[...]), wo_ref[w:2 * w, :], preferred_element_type=F32))
    x1 = x_ref[...] + g1_ref[...] * merged
    x1_ref[...] = x1
    r2 = lax.rsqrt(jnp.mean(x1 * x1, axis=-1, keepdims=True) + EPS)
    h2 = (x1 * r2) * n2_ref[...] * (1.0 + sc2_ref[...]) + sh2_ref[...]
    hi, lo = _split_bf16(h2)
    hhi_ref[...] = hi
    hlo_ref[...] = lo


def _post_call(attn, gb, u, prev0, prev1, x, g1, sc2, sh2, conv_w8, gn_attn, gn_conv, norm2, gmat, w_o_b,
               tm, group, prev_from_u):
    n, d = x.shape
    w = attn.shape[1]
    row = lambda i: (i, 0)
    fixed = lambda i: (0, 0)
    per_row = g1.shape[0] != 1
    mod_spec = pl.BlockSpec((tm, d), row) if per_row else pl.BlockSpec((1, d), fixed)
    if prev_from_u:
        sub = tm // 8
        p0_spec = pl.BlockSpec((8, w), fixed)
        p1_spec = pl.BlockSpec((8, w), lambda i: (jnp.maximum(i * sub - 1, 0), 0))
    else:
        p0_spec = pl.BlockSpec((tm, w), row)
        p1_spec = pl.BlockSpec((tm, w), row)
    return pl.pallas_call(
        functools.partial(_post_kernel, group=group, prev_from_u=prev_from_u),
        grid=(n // tm,),
        in_specs=[pl.BlockSpec((tm, w), row), pl.BlockSpec((tm, w), row), pl.BlockSpec((tm, w), row),
                  p0_spec, p1_spec, pl.BlockSpec((tm, d), row), mod_spec, mod_spec, mod_spec,
                  pl.BlockSpec((8, w), fixed), pl.BlockSpec((1, w), fixed), pl.BlockSpec((1, w), fixed),
                  pl.BlockSpec((1, d), fixed), pl.BlockSpec((w, w), fixed), pl.BlockSpec((2 * w, d), fixed)],
        out_specs=(pl.BlockSpec((tm, d), row), pl.BlockSpec((tm, d), row), pl.BlockSpec((tm, d), row)),
        out_shape=(jax.ShapeDtypeStruct((n, d), F32), jax.ShapeDtypeStruct((n, d), BF16),
                   jax.ShapeDtypeStruct((n, d), BF16)),
        compiler_params=_cparams(1),
    )(attn, gb, u, prev0, prev1, x, g1, sc2, sh2, conv_w8, gn_attn, gn_conv, norm2, gmat, w_o_b)


def _extract_topk(work, n_rounds, vals_ref=None):
    rows = lax.broadcasted_iota(jnp.int32, work.shape, 0).astype(F32)
    rank = jnp.full(work.shape, NOT_RANKED, F32)
    big = float(work.shape[0])
    for r in range(n_rounds):
        m = jnp.max(work, axis=0, keepdims=True)
        first = jnp.min(jnp.where(work == m, rows, big), axis=0, keepdims=True)
        sel = rows == first
        rank = jnp.where(sel, float(r + 1), rank)
        work = jnp.where(sel, NEG_INF, work)
        if vals_ref is not None:
            vals_ref[r:r + 1, :] = m
    return rank


def _candidate_pairs():
    return [(a, b) for a in range(PEER_TOPK) for b in range(PEER_TOPK) if (a + 1) * (b + 1) <= PEER_TOPK]


def _route_kernel(hhi_ref, hlo_ref, wqh_ref, wql_ref, skh_ref, skl_ref,
                  rank1_ref, cnt_ref, e0_ref, e1_ref, va_s, vb_s, cand_s, sel_s, *, n_cand_rows):
    hhi = hhi_ref[...]
    q = (jnp.dot(hhi, wqh_ref[...], preferred_element_type=F32)
         + jnp.dot(hhi, wql_ref[...], preferred_element_type=F32)
         + jnp.dot(hlo_ref[...], wqh_ref[...], preferred_element_type=F32))
    scores = []
    for p in range(2):
        qh, ql = _split_bf16(q[:, p * N_KEYS:(p + 1) * N_KEYS])
        skh = skh_ref[0, p]
        scores.append(_nt_dot(skh, qh) + _nt_dot(skh, ql) + _nt_dot(skl_ref[0, p], qh))
    rank0 = _extract_topk(scores[0], PEER_TOPK, va_s)
    rank1 = _extract_topk(scores[1], PEER_TOPK, vb_s)

    pairs = _candidate_pairs()
    cand_s[...] = jnp.full(cand_s.shape, NEG_INF, F32)
    for idx, (a, b) in enumerate(pairs):
        cand_s[idx:idx + 1, :] = va_s[a:a + 1, :] + vb_s[b:b + 1, :]
    sel_s[...] = jnp.where(_extract_topk(cand_s[...], PEER_TOPK) <= float(PEER_TOPK), 1.0, 0.0)

    a_max = va_s[0:1, :]
    b_max = vb_s[0:1, :]
    tm = a_max.shape[1]
    zsum = jnp.zeros((1, tm), F32)
    cnts = [jnp.zeros((1, tm), F32) for _ in range(PEER_TOPK)]
    for idx, (a, b) in enumerate(pairs):
        s_row = sel_s[idx:idx + 1, :]
        cnts[a] = cnts[a] + s_row
        zsum = zsum + s_row * (jnp.exp(va_s[a:a + 1, :] - a_max) * jnp.exp(vb_s[b:b + 1, :] - b_max))
    cnt = jnp.zeros(rank0.shape, F32)
    for a in range(PEER_TOPK):
        cnt = jnp.where(rank0 == float(a + 1), cnts[a], cnt)
    in0 = rank0 <= float(PEER_TOPK)
    in1 = rank1 <= float(PEER_TOPK)
    rank1_ref[0] = rank1
    cnt_ref[0] = cnt
    e0_ref[0] = jnp.where(in0, jnp.exp(jnp.minimum(scores[0] - a_max, 0.0)), 0.0) * (1.0 / zsum)
    e1_ref[0] = jnp.where(in1, jnp.exp(jnp.minimum(scores[1] - b_max, 0.0)), 0.0)


def _route_call(hhi, hlo, wq_hi, wq_lo, sk_hi, sk_lo, tm):
    n, d = hhi.shape
    qd = 2 * N_KEYS
    n_cand_rows = 56
    tab = jax.ShapeDtypeStruct((PEER_HEADS, N_KEYS, n), F32)
    tab_spec = pl.BlockSpec((1, N_KEYS, tm), lambda i, h: (h, 0, i))
    row = lambda i, h: (i, 0)
    return pl.pallas_call(
        functools.partial(_route_kernel, n_cand_rows=n_cand_rows),
        grid=(n // tm, PEER_HEADS),
        in_specs=[pl.BlockSpec((tm, d), row), pl.BlockSpec((tm, d), row),
                  pl.BlockSpec((d, qd), lambda i, h: (0, h)), pl.BlockSpec((d, qd), lambda i, h: (0, h)),
                  pl.BlockSpec((1, 2, N_KEYS, N_KEYS), lambda i, h: (h, 0, 0, 0)),
                  pl.BlockSpec((1, 2, N_KEYS, N_KEYS), lambda i, h: (h, 0, 0, 0))],
        out_specs=(tab_spec, tab_spec, tab_spec, tab_spec),
        out_shape=(tab, tab, tab, tab),
        scratch_shapes=[pltpu.VMEM((PEER_TOPK, tm), F32), pltpu.VMEM((PEER_TOPK, tm), F32),
                        pltpu.VMEM((n_cand_rows, tm), F32), pltpu.VMEM((n_cand_rows, tm), F32)],
        compiler_params=_cparams(2),
    )(hhi, hlo, wq_hi, wq_lo, sk_hi, sk_lo)


def _gelu_tanh(x):
    return 0.5 * x * (1.0 + jnp.tanh(0.7978845608028654 * (x + 0.044715 * (x * x * x))))


def _dense_kernel(hhi_ref, u_ref, v_ref, rank1_ref, cnt_ref, e0_ref, e1_ref, x1_ref, g2_ref, nf_ref,
                  y_ref, acc_s, wt_s, *, te):
    eb = pl.program_id(1)
    n_eb = pl.num_programs(1)
    tn = hhi_ref.shape[0]

    @pl.when(eb == 0)
    def _():
        acc_s[...] = jnp.zeros_like(acc_s)

    act_t = _nt_dot(u_ref[...], hhi_ref[...])
    ipb = te // N_KEYS
    upper_half = (eb & 1) == 1
    for ii in range(ipb):
        for lc in range(tn // LANES):
            ls = slice(lc * LANES, (lc + 1) * LANES)
            g = jnp.zeros((N_KEYS, LANES), F32)
            for h in range(PEER_HEADS):
                cnt_row = jnp.where(upper_half, cnt_ref[h, ipb + ii:ipb + ii + 1, ls], cnt_ref[h, ii:ii + 1, ls])
                e0_row = jnp.where(upper_half, e0_ref[h, ipb + ii:ipb + ii + 1, ls], e0_ref[h, ii:ii + 1, ls])
                g = g + jnp.where(rank1_ref[h, :, ls] <= cnt_row, e1_ref[h, :, ls], 0.0) * e0_row
            rs = slice(ii * N_KEYS, (ii + 1) * N_KEYS)
            wt_s[rs, ls] = g * _gelu_tanh(act_t[rs, ls])
    acc_s[...] += jnp.dot(wt_s[...].T.astype(BF16), v_ref[...], preferred_element_type=F32)

    @pl.when(eb == n_eb - 1)
    def _():
        x = x1_ref[...] + g2_ref[...] * acc_s[...]
        r = lax.rsqrt(jnp.mean(x * x, axis=-1, keepdims=True) + EPS)
        y_ref[...] = (x * r) * nf_ref[...]


def _dense_call(hhi, u_b, v_b, rank1, cnt, e0, e1, x1, g2, norm_f, tn, te):
    n, d = hhi.shape
    n_exp = u_b.shape[0]
    per_row = g2.shape[0] != 1
    row = lambda t, e: (t, 0)
    mod_spec = pl.BlockSpec((tn, d), row) if per_row else pl.BlockSpec((1, d), lambda t, e: (0, 0))
    tab_spec = pl.BlockSpec((PEER_HEADS, N_KEYS, tn), lambda t, e: (0, 0, t))
    assert 2 * te // N_KEYS == 8
    grp_spec = pl.BlockSpec((PEER_HEADS, 8, tn), lambda t, e: (0, e // 2, t))
    return pl.pallas_call(
        functools.partial(_dense_kernel, te=te),
        grid=(n // tn, n_exp // te),
        in_specs=[pl.BlockSpec((tn, d), row),
                  pl.BlockSpec((te, d), lambda t, e: (e, 0)), pl.BlockSpec((te, d), lambda t, e: (e, 0)),
                  tab_spec, grp_spec, grp_spec, tab_spec,
                  pl.BlockSpec((tn, d), row), mod_spec, pl.BlockSpec((1, d), lambda t, e: (0, 0))],
        out_specs=pl.BlockSpec((tn, d), row),
        out_shape=jax.ShapeDtypeStruct((n, d), F32),
        scratch_shapes=[pltpu.VMEM((tn, d), F32), pltpu.VMEM((te, tn), F32)],
        compiler_params=_cparams(2),
    )(hhi, u_b, v_b, rank1, cnt, e0, e1, x1, g2, norm_f)


def _peer_and_norm(hhi, hlo, x1, g2, weights, tm_route, tn_dense):
    wq_hi, wq_lo, sk_hi, sk_lo, u_b, v_b, norm_f = weights
    rank1, cnt, e0, e1 = _route_call(hhi, hlo, wq_hi, wq_lo, sk_hi, sk_lo, tm_route)
    return _dense_call(hhi, u_b, v_b, rank1, cnt, e0, e1, x1, g2, norm_f, tn_dense, DENSE_EXPERTS)


def kernel(x_prompt, x_sample, cache_k, cache_v, state_conv, page_table, c_prompt, c_sample, w_ada, b_ada,
           norm1, w_in, sb_bias, conv_w, gn_attn, gn_conv, w_o, norm2, w_query, sub_keys, expert_u, expert_v,
           norm_f):
    depth = w_in.shape[0]
    assert depth == 1 and x_prompt.shape[0] == 1
    d = x_prompt.shape[-1]
    seq = x_prompt.shape[1]
    nb, tq = x_sample.shape[:2]
    ns = nb * tq
    w = ATT_WIDTH
    l = 0

    c_all = jnp.concatenate([c_prompt, c_sample], axis=0)
    pad = (-c_all.shape[0]) % 8
    c_all = jnp.pad(c_all, ((0, pad), (0, 0)))
    mod = _mod_call(c_all, w_ada[l], b_ada[l])
    mods_p = [mod[0:1, k * d:(k + 1) * d] for k in range(6)]
    mods_s = [jnp.repeat(mod[1:1 + nb, k * d:(k + 1) * d], tq, axis=0) for k in range(6)]

    w_in_b = w_in[l].astype(BF16)
    w_o_b = w_o[l].astype(BF16)
    wq = w_query[l]
    wq_hi = wq.astype(BF16)
    wq_lo = (wq - wq_hi.astype(F32)).astype(BF16)
    sk = sub_keys[l]
    sk_hi = sk.astype(BF16)
    sk_lo = (sk - sk_hi.astype(F32)).astype(BF16)
    peer_w = (wq_hi, wq_lo, sk_hi, sk_lo, expert_u[l].astype(BF16), expert_v[l].astype(BF16),
              norm_f.reshape(1, d))
    grp = jnp.arange(w, dtype=jnp.int32) // HEAD_DIM
    gmat = jnp.where(grp[:, None] == grp[None, :], 1.0 / HEAD_DIM, 0.0).astype(BF16)
    conv_w8 = jnp.pad(conv_w[l], ((0, 8 - conv_w.shape[1]), (0, 0)))
    n1 = norm1[l].reshape(1, d)
    n2 = norm2[l].reshape(1, d)
    gna = gn_attn[l].reshape(1, w)
    gnc = gn_conv[l].reshape(1, w)
    bias = sb_bias[l].astype(F32)

    xp = x_prompt.reshape(seq, d)
    qs, k_p, kb, v_p, vt, gb, u = _proj_call(xp, mods_p[0], mods_p[1], n1, w_in_b, PROJ_ROWS, ATT_TILE)
    attn = _attn_prompt_call(bias, qs, kb, vt, ATT_TILE)
    zero_buf = jnp.zeros((8, w), F32)
    x1, hhi, hlo = _post_call(attn, gb, u, zero_buf, u, xp, mods_p[2], mods_p[4], mods_p[3], conv_w8,
                              gna, gnc, n2, gmat, w_o_b, POST_ROWS, POST_ROWS, True)
    y_p = _peer_and_norm(hhi, hlo, x1, mods_p[5], peer_w, ROUTE_ROWS, DENSE_ROWS)

    xs = x_sample.reshape(ns, d)
    qs_s, k_s, _, v_s, _, gb_s, u_s = _proj_call(xs, mods_s[0], mods_s[1], n1, w_in_b, ns, ns)
    page = cache_k.shape[2]
    ck = cache_k[l].reshape(-1, page, w)
    cv = cache_v[l].reshape(-1, page, w)
    bias_cols = jnp.pad(jnp.tile(bias, tq), (0, LANES - tq * ATT_HEADS)).reshape(1, LANES)
    q_s = qs_s.astype(F32)
    attn_s = _attn_sample_call(page_table, bias_cols, q_s.reshape(nb, tq, w), k_s.reshape(nb, tq, w),
                               v_s.reshape(nb, tq, w), ck, cv).reshape(ns, w)
    st = state_conv[l]
    prev0 = jnp.repeat(st[:, 0, :], tq, axis=0)
    prev1 = jnp.repeat(st[:, 1, :], tq, axis=0)
    x1_s, hhi_s, hlo_s = _post_call(attn_s, gb_s, u_s, prev0, prev1, xs, mods_s[2], mods_s[4], mods_s[3],
                                    conv_w8, gna, gnc, n2, gmat, w_o_b, ns, tq, False)
    y_s = _peer_and_norm(hhi_s, hlo_s, x1_s, mods_s[5], peer_w, ns, ns)

    heads = (ATT_HEADS, HEAD_DIM)
    return (y_p.reshape(1, seq, d),
            y_s.reshape(nb, tq, d),
            k_p.reshape(1, 1, seq, *heads),
            v_p.reshape(1, 1, seq, *heads),
            u[seq - 2:seq].reshape(1, 1, 2, w),
            k_s.reshape(1, nb, tq, *heads),
            v_s.reshape(1, nb, tq, *heads),
            u_s.reshape(nb, tq, w)[:, tq - 2:tq].reshape(1, nb, 2, w))
```

```python
import functools

import jax
import jax.numpy as jnp
from jax import lax
from jax.experimental import pallas as pl
from jax.experimental.pallas import tpu as pltpu

F32 = jnp.float32
BF16 = jnp.bfloat16
EPS = 1e-6

ATT_HEADS = 16
HEAD_DIM = 64
ATT_WIDTH = ATT_HEADS * HEAD_DIM
SB_SCALE = HEAD_DIM ** -0.5
PEER_HEADS = 8
N_KEYS = 128
PEER_TOPK = 16
NOT_RANKED = 99.0
NEG_INF = float("-inf")

VMEM_LIMIT_BYTES = 56 * 1024 * 1024
LANES = 128

ATT_TILE = 256
ATT_GROUP = 4
ATT_HEADS_PER_STEP = 4
PROJ_ROWS = 256
POST_ROWS = 256
ROUTE_ROWS = 512
DENSE_ROWS = 512
DENSE_EXPERTS = 512


def _cparams(n_axes):
    return pltpu.CompilerParams(dimension_semantics=("arbitrary",) * n_axes,
                                vmem_limit_bytes=VMEM_LIMIT_BYTES)


def _nt_dot(a, b):
    return lax.dot_general(a, b, (((1,), (1,)), ((), ())), preferred_element_type=F32)


def _split_bf16(x):
    hi = x.astype(BF16)
    lo = (x - hi.astype(F32)).astype(BF16)
    return hi, lo


def _mod_kernel(c_ref, w_ref, b_ref, o_ref):
    c = c_ref[...]
    s = c * (1.0 / (1.0 + jnp.exp(-c)))
    s_hi, s_lo = _split_bf16(s)
    w = w_ref[...]
    w_hi, w_lo = _split_bf16(w)
    acc = jnp.dot(s_hi, w_hi, preferred_element_type=F32)
    acc += jnp.dot(s_hi, w_lo, preferred_element_type=F32)
    acc += jnp.dot(s_lo, w_hi, preferred_element_type=F32)
    o_ref[...] = acc + b_ref[...]


def _mod_call(c, w_ada, b_ada):
    m, d = c.shape
    n = w_ada.shape[1]
    tn = 512
    return pl.pallas_call(
        _mod_kernel,
        grid=(n // tn,),
        in_specs=[pl.BlockSpec((m, d), lambda j: (0, 0)),
                  pl.BlockSpec((d, tn), lambda j: (0, j)),
                  pl.BlockSpec((1, tn), lambda j: (0, j))],
        out_specs=pl.BlockSpec((m, tn), lambda j: (0, j)),
        out_shape=jax.ShapeDtypeStruct((m, n), F32),
        compiler_params=_cparams(1),
    )(c, w_ada, b_ada.reshape(1, n))


def _proj_kernel(x_ref, sh_ref, sc_ref, g_ref, w_ref,
                 q_ref, k_ref, kb_ref, v_ref, vt_ref, gb_ref, u_ref, h_s, gc_s, *, tk):
    j = pl.program_id(1)

    @pl.when(j == 0)
    def _():
        x = x_ref[...]
        r = lax.rsqrt(jnp.mean(x * x, axis=-1, keepdims=True) + EPS)
        h = (x * r) * g_ref[...] * (1.0 + sc_ref[...]) + sh_ref[...]
        h_s[...] = h.astype(BF16)

    p = jnp.dot(h_s[...], w_ref[...], preferred_element_type=F32)

    @pl.when(j == 0)
    def _():
        q_ref[...] = (p * SB_SCALE).astype(BF16)

    @pl.when(j == 1)
    def _():
        k_ref[...] = p
        kb_ref[...] = p.astype(BF16)

    @pl.when(j == 2)
    def _():
        v_ref[...] = p
        for c in range(p.shape[0] // tk):
            vt_ref[c] = p[c * tk:(c + 1) * tk, :].T.astype(BF16)

    @pl.when(j == 3)
    def _():
        gb_ref[...] = p

    @pl.when(j == 4)
    def _():
        gc_s[...] = p

    @pl.when(j == 5)
    def _():
        u_ref[...] = gc_s[...] * p


def _proj_call(x, sh, sc, norm_g, w_in_b, tm, tk):
    n, d = x.shape
    w = ATT_WIDTH
    per_row = sh.shape[0] != 1
    mod_spec = (pl.BlockSpec((tm, d), lambda i, j: (i, 0)) if per_row
                else pl.BlockSpec((1, d), lambda i, j: (0, 0)))
    row = lambda i, j: (i, 0)
    outs = (
        jax.ShapeDtypeStruct((n, w), BF16),
        jax.ShapeDtypeStruct((n, w), F32),
        jax.ShapeDtypeStruct((n, w), BF16),
        jax.ShapeDtypeStruct((n, w), F32),
        jax.ShapeDtypeStruct((n // tk, w, tk), BF16),
        jax.ShapeDtypeStruct((n, w), F32),
        jax.ShapeDtypeStruct((n, w), F32),
    )
    return pl.pallas_call(
        functools.partial(_proj_kernel, tk=tk),
        grid=(n // tm, 6),
        in_specs=[pl.BlockSpec((tm, d), row), mod_spec, mod_spec,
                  pl.BlockSpec((1, d), lambda i, j: (0, 0)),
                  pl.BlockSpec((d, w), lambda i, j: (0, j))],
        out_specs=(pl.BlockSpec((tm, w), row), pl.BlockSpec((tm, w), row), pl.BlockSpec((tm, w), row),
                   pl.BlockSpec((tm, w), row), pl.BlockSpec((tm // tk, w, tk), lambda i, j: (i, 0, 0)),
                   pl.BlockSpec((tm, w), row), pl.BlockSpec((tm, w), row)),
        out_shape=outs,
        scratch_shapes=[pltpu.VMEM((tm, d), BF16), pltpu.VMEM((tm, w), F32)],
        compiler_params=_cparams(2),
    )(x, sh, sc, norm_g, w_in_b)


def _softplus_clamped(z):
    return jnp.maximum(z, jnp.log(1.0 + jnp.exp(jnp.minimum(z, 30.0))))


def _attn_prompt_kernel(bias_ref, q_ref, kb_ref, vt_ref, o_ref, acc_s, *, t, g, nh):
    hg = pl.program_id(0)
    i = pl.program_id(1)
    lane = lax.broadcasted_iota(jnp.int32, (t, LANES), 1)
    q_heads = []
    for h in range(nh):
        q2 = q_ref[:, (h // 2) * LANES:(h // 2 + 1) * LANES]
        keep = (lane < HEAD_DIM) if h % 2 == 0 else (lane >= HEAD_DIM)
        q_heads.append(jnp.where(keep, q2, jnp.zeros_like(q2)))
    biases = [bias_ref[nh * hg + h] for h in range(nh)]
    r_io = lax.broadcasted_iota(jnp.int32, (t, t), 0)
    c_io = lax.broadcasted_iota(jnp.int32, (t, t), 1)
    tri = jnp.where(r_io < c_io, 1.0, 0.0).astype(BF16)
    key_minus_query = r_io - c_io
    acc_s[...] = jnp.zeros_like(acc_s)

    def group(gi, carry, masked):
        run = list(carry)
        tiles = [(b, h) for b in reversed(range(g)) for h in range(nh)]
        st = [dict() for _ in tiles]

        def scores(k):
            b, h = tiles[k]
            blk = gi * g + b
            start = pl.multiple_of(blk * t, t)
            kblk = kb_ref[pl.ds(start, t), (h // 2) * LANES:(h // 2 + 1) * LANES]
            st[k]["z"] = _nt_dot(kblk, q_heads[h]) + biases[h]
            if masked:
                st[k]["vis"] = key_minus_query < (i - blk) * t

        def suffix(k):
            z = st[k]["z"]
            sp = _softplus_clamped(z)
            spm = jnp.where(st[k]["vis"], sp, 0.0) if masked else sp
            st[k]["lb"] = z - sp
            st[k]["row0"] = spm[0:1, :]
            st[k]["suf"] = jnp.dot(tri, spm.astype(BF16), preferred_element_type=F32)

        def values(k):
            b, h = tiles[k]
            suf = st[k]["suf"]
            a = jnp.exp(st[k]["lb"] - suf)
            if masked:
                a = jnp.where(st[k]["vis"], a, 0.0)
            vtb = vt_ref[gi * g + b, h * HEAD_DIM:(h + 1) * HEAD_DIM, :]
            st[k]["o"] = jnp.dot(vtb, a.astype(BF16), preferred_element_type=F32)
            st[k]["tot"] = suf[0:1, :] + st[k]["row0"]

        def accumulate(k):
            b, h = tiles[k]
            rows = slice(h * HEAD_DIM, (h + 1) * HEAD_DIM)
            acc_s[rows, :] += st[k]["o"] * jnp.exp(-run[h])
            run[h] = run[h] + st[k]["tot"]
            st[k].clear()

        stages = ((scores, 0), (suffix, 2), (values, 4), (accumulate, 5))
        for step in range(len(tiles) + stages[-1][1]):
            for fn, lag in stages:
                if 0 <= step - lag < len(tiles):
                    fn(step - lag)
        return tuple(run)

    c0 = jnp.zeros((1, t), F32)
    top = i // g
    carry = group(top, (c0,) * nh, True)
    lax.fori_loop(0, top, lambda s, c: group(top - 1 - s, c, False), carry)
    o_ref[...] = acc_s[...].T


def _attn_prompt_call(bias, qs, kb, vt, t, g, nh):
    n, w = qs.shape
    nk = n // t
    wb = nh * HEAD_DIM
    assert nk % g == 0 and wb % LANES == 0
    return pl.pallas_call(
        functools.partial(_attn_prompt_kernel, t=t, g=g, nh=nh),
        grid=(w // wb, n // t),
        in_specs=[pl.BlockSpec(memory_space=pltpu.SMEM),
                  pl.BlockSpec((t, wb), lambda hg, i: (i, hg)),
                  pl.BlockSpec((n, wb), lambda hg, i: (0, hg)),
                  pl.BlockSpec((nk, wb, t), lambda hg, i: (0, hg, 0))],
        out_specs=pl.BlockSpec((t, wb), lambda hg, i: (i, hg)),
        out_shape=jax.ShapeDtypeStruct((n, w), F32),
        scratch_shapes=[pltpu.VMEM((wb, t), F32)],
        compiler_params=_cparams(2),
    )(bias, qs, kb, vt)


def _attn_sample_kernel(pt_ref, bias_ref, q_ref, kn_ref, vn_ref, cka_ref, cva_ref, ckb_ref, cvb_ref, o_ref,
                        qbd_s, acc_s, carry_s, *, tq, page):
    s = pl.program_id(1)
    n_steps = pl.num_programs(1)
    w = ATT_WIDTH
    nr = tq * ATT_HEADS
    r_io = lax.broadcasted_iota(jnp.int32, (page, page), 0)
    c_io = lax.broadcasted_iota(jnp.int32, (page, page), 1)
    tri = jnp.where(r_io > c_io, 1.0, 0.0).astype(BF16)
    ones = jnp.ones((page, page), BF16)
    head_of_lane = lax.shift_right_logical(lax.broadcasted_iota(jnp.int32, (ATT_HEADS, w), 1), 6)
    head_sel = head_of_lane == lax.broadcasted_iota(jnp.int32, (ATT_HEADS, w), 0)

    def scores(kt):
        return jnp.dot(qbd_s[...], kt.astype(BF16), preferred_element_type=F32) + bias_ref[...]

    def local(z, mask):
        sp = _softplus_clamped(z)
        spm = sp if mask is None else jnp.where(mask, sp, 0.0)
        spb = spm.astype(BF16)
        return (z - sp, jnp.dot(spb, tri, preferred_element_type=F32),
                jnp.dot(spb, ones, preferred_element_type=F32))

    def weights(log_beta, suf, carry, mask):
        a = jnp.exp(log_beta - suf - carry)
        if mask is not None:
            a = jnp.where(mask, a, 0.0)
        return a.astype(BF16)

    @pl.when(s == 0)
    def _():
        q4 = q_ref[0]
        for qi in range(tq):
            blk = jnp.where(head_sel, jnp.broadcast_to(q4[qi:qi + 1, :], (ATT_HEADS, w)), 0.0)
            qbd_s[qi * ATT_HEADS:(qi + 1) * ATT_HEADS, :] = blk.astype(BF16)
        pad = jnp.zeros((page - tq, w), F32)
        kt = jnp.concatenate([kn_ref[0], pad], axis=0).T
        vt = jnp.concatenate([vn_ref[0], pad], axis=0).T
        key_lane = lax.broadcasted_iota(jnp.int32, (nr, page), 1)
        q_of_row = lax.shift_right_logical(lax.broadcasted_iota(jnp.int32, (nr, page), 0), 4)
        mask = key_lane < q_of_row
        log_beta, suf, tot = local(scores(kt), mask)
        a = weights(log_beta, suf, jnp.zeros_like(suf), mask)
        acc_s[...] = _nt_dot(a, vt.astype(BF16))
        carry_s[...] = tot

    @pl.when(s > 0)
    def _():
        za = scores(cka_ref[0])
        zb = scores(ckb_ref[0])
        lba, sufa, tota = local(za, None)
        lbb, sufb, totb = local(zb, None)
        c = carry_s[...]
        aa = weights(lba, sufa, c, None)
        ab = weights(lbb, sufb, c + tota, None)
        acc_s[...] += _nt_dot(aa, cva_ref[0].astype(BF16)) + _nt_dot(ab, cvb_ref[0].astype(BF16))
        carry_s[...] = c + tota + totb

    @pl.when(s == n_steps - 1)
    def _():
        for qi in range(tq):
            blk = acc_s[qi * ATT_HEADS:(qi + 1) * ATT_HEADS, :]
            o_ref[0, qi:qi + 1, :] = jnp.sum(jnp.where(head_sel, blk, 0.0), axis=0, keepdims=True)


def _attn_sample_call(page_table, bias_rows, q, k_new, v_new, cache_kt, cache_vt):
    b, tq, w = q.shape
    n_pages = page_table.shape[1]
    page = cache_kt.shape[2]
    nr = tq * ATT_HEADS
    assert nr % 16 == 0 and page == LANES and n_pages % 2 == 0

    def near_idx(bi, s, pt):
        return (pt[bi * n_pages + jnp.clip(n_pages + 1 - 2 * s, 1, n_pages - 1)], 0, 0)

    def far_idx(bi, s, pt):
        return (pt[bi * n_pages + jnp.clip(n_pages - 2 * s, 0, n_pages - 2)], 0, 0)

    new_spec = pl.BlockSpec((1, tq, w), lambda bi, s, pt: (bi, 0, 0))
    grid_spec = pltpu.PrefetchScalarGridSpec(
        num_scalar_prefetch=1,
        grid=(b, n_pages // 2 + 1),
        in_specs=[pl.BlockSpec((nr, page), lambda bi, s, pt: (0, 0)),
                  new_spec, new_spec, new_spec,
                  pl.BlockSpec((1, w, page), near_idx), pl.BlockSpec((1, w, page), near_idx),
                  pl.BlockSpec((1, w, page), far_idx), pl.BlockSpec((1, w, page), far_idx)],
        out_specs=new_spec,
        scratch_shapes=[pltpu.VMEM((nr, w), BF16), pltpu.VMEM((nr, w), F32), pltpu.VMEM((nr, page), F32)],
    )
    return pl.pallas_call(
        functools.partial(_attn_sample_kernel, tq=tq, page=page),
        grid_spec=grid_spec,
        out_shape=jax.ShapeDtypeStruct((b, tq, w), F32),
        compiler_params=_cparams(2),
    )(page_table.reshape(-1), bias_rows, q, k_new, v_new, cache_kt, cache_vt, cache_kt, cache_vt)


def _post_kernel(attn_ref, gb_ref, u_ref, p0_ref, p1_ref, x_ref, g1_ref, sc2_ref, sh2_ref,
                 cw_ref, gna_ref, gnc_ref, n2_ref, gm_ref, wo_ref,
                 x1_ref, hhi_ref, hlo_ref, *, group, prev_from_u):
    i = pl.program_id(0)
    u = u_ref[...]
    assert group & (group - 1) == 0
    if prev_from_u:
        first = i == 0
        buf = p0_ref[...]
        prev = p1_ref[...]
        p0 = jnp.where(first, buf[6:7, :], prev[6:7, :])
        p1 = jnp.where(first, buf[7:8, :], prev[7:8, :])
    else:
        p0 = p0_ref[...]
        p1 = p1_ref[...]
    r = lax.broadcasted_iota(jnp.int32, u.shape, 0) & (group - 1)
    um1 = jnp.where(r == 0, p1, pltpu.roll(u, 1, 0))
    um2 = jnp.where(r == 0, p0, jnp.where(r == 1, p1, pltpu.roll(u, 2, 0)))
    cw = cw_ref[...]
    conv = cw[0:1, :] * um2 + cw[1:2, :] * um1 + cw[2:3, :] * u
    yc = gb_ref[...] * conv
    attn = attn_ref[...]
    gm = gm_ref[...]

    def gnorm(v, g):
        msq = jnp.dot((v * v).astype(BF16), gm, preferred_element_type=F32)
        return ((v * lax.rsqrt(msq + EPS)) * g).astype(BF16)

    w = attn.shape[1]
    merged = (jnp.dot(gnorm(attn, gna_ref[...]), wo_ref[0:w, :], preferred_element_type=F32)
              + jnp.dot(gnorm(yc, gnc_ref[...]), wo_ref[w:2 * w, :], preferred_element_type=F32))
    x1 = x_ref[...] + g1_ref[...] * merged
    x1_ref[...] = x1
    r2 = lax.rsqrt(jnp.mean(x1 * x1, axis=-1, keepdims=True) + EPS)
    h2 = (x1 * r2) * n2_ref[...] * (1.0 + sc2_ref[...]) + sh2_ref[...]
    hi, lo = _split_bf16(h2)
    hhi_ref[...] = hi
    hlo_ref[...] = lo


def _post_call(attn, gb, u, prev0, prev1, x, g1, sc2, sh2, conv_w8, gn_attn, gn_conv, norm2, gmat, w_o_b,
               tm, group, prev_from_u):
    n, d = x.shape
    w = attn.shape[1]
    row = lambda i: (i, 0)
    fixed = lambda i: (0, 0)
    per_row = g1.shape[0] != 1
    mod_spec = pl.BlockSpec((tm, d), row) if per_row else pl.BlockSpec((1, d), fixed)
    if prev_from_u:
        sub = tm // 8
        p0_spec = pl.BlockSpec((8, w), fixed)
        p1_spec = pl.BlockSpec((8, w), lambda i: (jnp.maximum(i * sub - 1, 0), 0))
    else:
        p0_spec = pl.BlockSpec((tm, w), row)
        p1_spec = pl.BlockSpec((tm, w), row)
    return pl.pallas_call(
        functools.partial(_post_kernel, group=group, prev_from_u=prev_from_u),
        grid=(n // tm,),
        in_specs=[pl.BlockSpec((tm, w), row), pl.BlockSpec((tm, w), row), pl.BlockSpec((tm, w), row),
                  p0_spec, p1_spec, pl.BlockSpec((tm, d), row), mod_spec, mod_spec, mod_spec,
                  pl.BlockSpec((8, w), fixed), pl.BlockSpec((1, w), fixed), pl.BlockSpec((1, w), fixed),
                  pl.BlockSpec((1, d), fixed), pl.BlockSpec((w, w), fixed), pl.BlockSpec((2 * w, d), fixed)],
        out_specs=(pl.BlockSpec((tm, d), row), pl.BlockSpec((tm, d), row), pl.BlockSpec((tm, d), row)),
        out_shape=(jax.ShapeDtypeStruct((n, d), F32), jax.ShapeDtypeStruct((n, d), BF16),
                   jax.ShapeDtypeStruct((n, d), BF16)),
        compiler_params=_cparams(1),
    )(attn, gb, u, prev0, prev1, x, g1, sc2, sh2, conv_w8, gn_attn, gn_conv, norm2, gmat, w_o_b)


def _extract_topk(work, n_rounds, vals_ref=None):
    rows = lax.broadcasted_iota(jnp.int32, work.shape, 0).astype(F32)
    rank = jnp.full(work.shape, NOT_RANKED, F32)
    big = float(work.shape[0])
    for r in range(n_rounds):
        m = jnp.max(work, axis=0, keepdims=True)
        first = jnp.min(jnp.where(work == m, rows, big), axis=0, keepdims=True)
        sel = rows == first
        rank = jnp.where(sel, float(r + 1), rank)
        work = jnp.where(sel, NEG_INF, work)
        if vals_ref is not None:
            vals_ref[r:r + 1, :] = m
    return rank


def _candidate_pairs():
    return [(a, b) for a in range(PEER_TOPK) for b in range(PEER_TOPK) if (a + 1) * (b + 1) <= PEER_TOPK]


def _route_kernel(hhi_ref, hlo_ref, wqh_ref, wql_ref, skh_ref, skl_ref,
                  rank1_ref, cnt_ref, e0_ref, e1_ref, va_s, vb_s, cand_s, sel_s, *, n_cand_rows):
    hhi = hhi_ref[...]
    q = (jnp.dot(hhi, wqh_ref[...], preferred_element_type=F32)
         + jnp.dot(hhi, wql_ref[...], preferred_element_type=F32)
         + jnp.dot(hlo_ref[...], wqh_ref[...], preferred_element_type=F32))
    scores = []
    for p in range(2):
        qh, ql = _split_bf16(q[:, p * N_KEYS:(p + 1) * N_KEYS])
        skh = skh_ref[0, p]
        scores.append(_nt_dot(skh, qh) + _nt_dot(skh, ql) + _nt_dot(skl_ref[0, p], qh))
    rank0 = _extract_topk(scores[0], PEER_TOPK, va_s)
    rank1 = _extract_topk(scores[1], PEER_TOPK, vb_s)

    pairs = _candidate_pairs()
    cand_s[...] = jnp.full(cand_s.shape, NEG_INF, F32)
    for idx, (a, b) in enumerate(pairs):
        cand_s[idx:idx + 1, :] = va_s[a:a + 1, :] + vb_s[b:b + 1, :]
    sel_s[...] = jnp.where(_extract_topk(cand_s[...], PEER_TOPK) <= float(PEER_TOPK), 1.0, 0.0)

    a_max = va_s[0:1, :]
    b_max = vb_s[0:1, :]
    tm = a_max.shape[1]
    zsum = jnp.zeros((1, tm), F32)
    cnts = [jnp.zeros((1, tm), F32) for _ in range(PEER_TOPK)]
    for idx, (a, b) in enumerate(pairs):
        s_row = sel_s[idx:idx + 1, :]
        cnts[a] = cnts[a] + s_row
        zsum = zsum + s_row * (jnp.exp(va_s[a:a + 1, :] - a_max) * jnp.exp(vb_s[b:b + 1, :] - b_max))
    cnt = jnp.zeros(rank0.shape, F32)
    for a in range(PEER_TOPK):
        cnt = jnp.where(rank0 == float(a + 1), cnts[a], cnt)
    in0 = rank0 <= float(PEER_TOPK)
    in1 = rank1 <= float(PEER_TOPK)
    rank1_ref[0] = rank1
    cnt_ref[0] = cnt
    e0_ref[0] = jnp.where(in0, jnp.exp(jnp.minimum(scores[0] - a_max, 0.0)), 0.0) * (1.0 / zsum)
    e1_ref[0] = jnp.where(in1, jnp.exp(jnp.minimum(scores[1] - b_max, 0.0)), 0.0)


def _route_call(hhi, hlo, wq_hi, wq_lo, sk_hi, sk_lo, tm):
    n, d = hhi.shape
    qd = 2 * N_KEYS
    n_cand_rows = 56
    tab = jax.ShapeDtypeStruct((PEER_HEADS, N_KEYS, n), F32)
    tab_spec = pl.BlockSpec((1, N_KEYS, tm), lambda i, h: (h, 0, i))
    row = lambda i, h: (i, 0)
    return pl.pallas_call(
        functools.partial(_route_kernel, n_cand_rows=n_cand_rows),
        grid=(n // tm, PEER_HEADS),
        in_specs=[pl.BlockSpec((tm, d), row), pl.BlockSpec((tm, d), row),
                  pl.BlockSpec((d, qd), lambda i, h: (0, h)), pl.BlockSpec((d, qd), lambda i, h: (0, h)),
                  pl.BlockSpec((1, 2, N_KEYS, N_KEYS), lambda i, h: (h, 0, 0, 0)),
                  pl.BlockSpec((1, 2, N_KEYS, N_KEYS), lambda i, h: (h, 0, 0, 0))],
        out_specs=(tab_spec, tab_spec, tab_spec, tab_spec),
        out_shape=(tab, tab, tab, tab),
        scratch_shapes=[pltpu.VMEM((PEER_TOPK, tm), F32), pltpu.VMEM((PEER_TOPK, tm), F32),
                        pltpu.VMEM((n_cand_rows, tm), F32), pltpu.VMEM((n_cand_rows, tm), F32)],
        compiler_params=_cparams(2),
    )(hhi, hlo, wq_hi, wq_lo, sk_hi, sk_lo)


def _gelu_tanh(x):
    return 0.5 * x * (1.0 + jnp.tanh(0.7978845608028654 * (x + 0.044715 * (x * x * x))))


def _dense_kernel(hhi_ref, u_ref, v_ref, rank1_ref, cnt_ref, e0_ref, e1_ref, x1_ref, g2_ref, nf_ref,
                  y_ref, acc_s, wt_s, *, te):
    eb = pl.program_id(1)
    n_eb = pl.num_programs(1)
    tn = hhi_ref.shape[0]

    @pl.when(eb == 0)
    def _():
        acc_s[...] = jnp.zeros_like(acc_s)

    act_t = _nt_dot(u_ref[...], hhi_ref[...])
    ipb = te // N_KEYS
    upper_half = (eb & 1) == 1
    for ii in range(ipb):
        for lc in range(tn // LANES):
            ls = slice(lc * LANES, (lc + 1) * LANES)
            g = jnp.zeros((N_KEYS, LANES), F32)
            for h in range(PEER_HEADS):
                cnt_row = jnp.where(upper_half, cnt_ref[h, ipb + ii:ipb + ii + 1, ls], cnt_ref[h, ii:ii + 1, ls])
                e0_row = jnp.where(upper_half, e0_ref[h, ipb + ii:ipb + ii + 1, ls], e0_ref[h, ii:ii + 1, ls])
                g = g + jnp.where(rank1_ref[h, :, ls] <= cnt_row, e1_ref[h, :, ls], 0.0) * e0_row
            rs = slice(ii * N_KEYS, (ii + 1) * N_KEYS)
            wt_s[rs, ls] = g * _gelu_tanh(act_t[rs, ls])
    acc_s[...] += jnp.dot(wt_s[...].T.astype(BF16), v_ref[...], preferred_element_type=F32)

    @pl.when(eb == n_eb - 1)
    def _():
        x = x1_ref[...] + g2_ref[...] * acc_s[...]
        r = lax.rsqrt(jnp.mean(x * x, axis=-1, keepdims=True) + EPS)
        y_ref[...] = (x * r) * nf_ref[...]


def _dense_call(hhi, u_b, v_b, rank1, cnt, e0, e1, x1, g2, norm_f, tn, te):
    n, d = hhi.shape
    n_exp = u_b.shape[0]
    per_row = g2.shape[0] != 1
    row = lambda t, e: (t, 0)
    mod_spec = pl.BlockSpec((tn, d), row) if per_row else pl.BlockSpec((1, d), lambda t, e: (0, 0))
    tab_spec = pl.BlockSpec((PEER_HEADS, N_KEYS, tn), lambda t, e: (0, 0, t))
    assert 2 * te // N_KEYS == 8
    grp_spec = pl.BlockSpec((PEER_HEADS, 8, tn), lambda t, e: (0, e // 2, t))
    return pl.pallas_call(
        functools.partial(_dense_kernel, te=te),
        grid=(n // tn, n_exp // te),
        in_specs=[pl.BlockSpec((tn, d), row),
                  pl.BlockSpec((te, d), lambda t, e: (e, 0)), pl.BlockSpec((te, d), lambda t, e: (e, 0)),
                  tab_spec, grp_spec, grp_spec, tab_spec,
                  pl.BlockSpec((tn, d), row), mod_spec, pl.BlockSpec((1, d), lambda t, e: (0, 0))],
        out_specs=pl.BlockSpec((tn, d), row),
        out_shape=jax.ShapeDtypeStruct((n, d), F32),
        scratch_shapes=[pltpu.VMEM((tn, d), F32), pltpu.VMEM((te, tn), F32)],
        compiler_params=_cparams(2),
    )(hhi, u_b, v_b, rank1, cnt, e0, e1, x1, g2, norm_f)


def _peer_and_norm(hhi, hlo, x1, g2, weights, tm_route, tn_dense):
    wq_hi, wq_lo, sk_hi, sk_lo, u_b, v_b, norm_f = weights
    rank1, cnt, e0, e1 = _route_call(hhi, hlo, wq_hi, wq_lo, sk_hi, sk_lo, tm_route)
    return _dense_call(hhi, u_b, v_b, rank1, cnt, e0, e1, x1, g2, norm_f, tn_dense, DENSE_EXPERTS)


def kernel(x_prompt, x_sample, cache_k, cache_v, state_conv, page_table, c_prompt, c_sample, w_ada, b_ada,
           norm1, w_in, sb_bias, conv_w, gn_attn, gn_conv, w_o, norm2, w_query, sub_keys, expert_u, expert_v,
           norm_f):
    depth = w_in.shape[0]
    assert depth == 1 and x_prompt.shape[0] == 1
    d = x_prompt.shape[-1]
    seq = x_prompt.shape[1]
    nb, tq = x_sample.shape[:2]
    ns = nb * tq
    w = ATT_WIDTH
    l = 0

    c_all = jnp.concatenate([c_prompt, c_sample], axis=0)
    pad = (-c_all.shape[0]) % 8
    c_all = jnp.pad(c_all, ((0, pad), (0, 0)))
    mod = _mod_call(c_all, w_ada[l], b_ada[l])
    mods_p = [mod[0:1, k * d:(k + 1) * d] for k in range(6)]
    mods_s = [jnp.repeat(mod[1:1 + nb, k * d:(k + 1) * d], tq, axis=0) for k in range(6)]

    w_in_b = w_in[l].astype(BF16)
    w_o_b = w_o[l].astype(BF16)
    wq = w_query[l]
    wq_hi = wq.astype(BF16)
    wq_lo = (wq - wq_hi.astype(F32)).astype(BF16)
    sk = sub_keys[l]
    sk_hi = sk.astype(BF16)
    sk_lo = (sk - sk_hi.astype(F32)).astype(BF16)
    peer_w = (wq_hi, wq_lo, sk_hi, sk_lo, expert_u[l].astype(BF16), expert_v[l].astype(BF16),
              norm_f.reshape(1, d))
    grp = jnp.arange(w, dtype=jnp.int32) // HEAD_DIM
    gmat = jnp.where(grp[:, None] == grp[None, :], 1.0 / HEAD_DIM, 0.0).astype(BF16)
    conv_w8 = jnp.pad(conv_w[l], ((0, 8 - conv_w.shape[1]), (0, 0)))
    n1 = norm1[l].reshape(1, d)
    n2 = norm2[l].reshape(1, d)
    gna = gn_attn[l].reshape(1, w)
    gnc = gn_conv[l].reshape(1, w)
    bias = sb_bias[l].astype(F32)

    xp = x_prompt.reshape(seq, d)
    qs, k_p, kb, v_p, vt, gb, u = _proj_call(xp, mods_p[0], mods_p[1], n1, w_in_b, PROJ_ROWS, ATT_TILE)
    attn = _attn_prompt_call(bias, qs, kb, vt, ATT_TILE, ATT_GROUP, ATT_HEADS_PER_STEP)
    zero_buf = jnp.zeros((8, w), F32)
    x1, hhi, hlo = _post_call(attn, gb, u, zero_buf, u, xp, mods_p[2], mods_p[4], mods_p[3], conv_w8,
                              gna, gnc, n2, gmat, w_o_b, POST_ROWS, POST_ROWS, True)
    y_p = _peer_and_norm(hhi, hlo, x1, mods_p[5], peer_w, ROUTE_ROWS, DENSE_ROWS)

    xs = x_sample.reshape(ns, d)
    qs_s, k_s, _, v_s, _, gb_s, u_s = _proj_call(xs, mods_s[0], mods_s[1], n1, w_in_b, ns, ns)
    ckt = jnp.transpose(cache_k[l], (0, 2, 3, 1)).reshape(cache_k.shape[1], w, cache_k.shape[2])
    cvt = jnp.transpose(cache_v[l], (0, 2, 3, 1)).reshape(cache_v.shape[1], w, cache_v.shape[2])
    bias_rows = jnp.broadcast_to(jnp.tile(bias, tq)[:, None], (tq * ATT_HEADS, LANES))
    q_s = qs_s.astype(F32)
    attn_s = _attn_sample_call(page_table, bias_rows, q_s.reshape(nb, tq, w), k_s.reshape(nb, tq, w),
                               v_s.reshape(nb, tq, w), ckt, cvt).reshape(ns, w)
    st = state_conv[l]
    prev0 = jnp.repeat(st[:, 0, :], tq, axis=0)
    prev1 = jnp.repeat(st[:, 1, :], tq, axis=0)
    x1_s, hhi_s, hlo_s = _post_call(attn_s, gb_s, u_s, prev0, prev1, xs, mods_s[2], mods_s[4], mods_s[3],
                                    conv_w8, gna, gnc, n2, gmat, w_o_b, ns, tq, False)
    y_s = _peer_and_norm(hhi_s, hlo_s, x1_s, mods_s[5], peer_w, ns, ns)

    heads = (ATT_HEADS, HEAD_DIM)
    return (y_p.reshape(1, seq, d),
            y_s.reshape(nb, tq, d),
            k_p.reshape(1, 1, seq, *heads),
            v_p.reshape(1, 1, seq, *heads),
            u[seq - 2:seq].reshape(1, 1, 2, w),
            k_s.reshape(1, nb, tq, *heads),
            v_s.reshape(1, nb, tq, *heads),
            u_s.reshape(nb, tq, w)[:, tq - 2:tq].reshape(1, nb, 2, w))
```

```python
import functools

import jax
import jax.numpy as jnp
from jax import lax
from jax.experimental import pallas as pl
from jax.experimental.pallas import tpu as pltpu

F32 = jnp.float32
BF16 = jnp.bfloat16
EPS = 1e-6

ATT_HEADS = 16
HEAD_DIM = 64
ATT_WIDTH = ATT_HEADS * HEAD_DIM
SB_SCALE = HEAD_DIM ** -0.5
PEER_HEADS = 8
N_KEYS = 128
PEER_TOPK = 16
NOT_RANKED = 99.0
NEG_INF = float("-inf")

VMEM_LIMIT_BYTES = 56 * 1024 * 1024
LANES = 128

ATT_TILE = 256
ATT_GROUP = 4
ATT_HEADS_PER_STEP = 8
PAGES_PER_STEP = 4
PROJ_ROWS = 512
POST_ROWS = 256
ROUTE_ROWS = 512
DENSE_ROWS = 512
DENSE_EXPERTS = 1024


def _cparams(n_axes):
    return pltpu.CompilerParams(dimension_semantics=("arbitrary",) * n_axes,
                                vmem_limit_bytes=VMEM_LIMIT_BYTES)


def _nt_dot(a, b):
    return lax.dot_general(a, b, (((1,), (1,)), ((), ())), preferred_element_type=F32)


def _split_bf16(x):
    hi = x.astype(BF16)
    lo = (x - hi.astype(F32)).astype(BF16)
    return hi, lo


def _mod_kernel(c_ref, w_ref, b_ref, o_ref):
    c = c_ref[...]
    s = c * (1.0 / (1.0 + jnp.exp(-c)))
    s_hi, s_lo = _split_bf16(s)
    w = w_ref[...]
    w_hi, w_lo = _split_bf16(w)
    acc = jnp.dot(s_hi, w_hi, preferred_element_type=F32)
    acc += jnp.dot(s_hi, w_lo, preferred_element_type=F32)
    acc += jnp.dot(s_lo, w_hi, preferred_element_type=F32)
    o_ref[...] = acc + b_ref[...]


def _mod_call(c, w_ada, b_ada):
    m, d = c.shape
    n = w_ada.shape[1]
    tn = 512
    return pl.pallas_call(
        _mod_kernel,
        grid=(n // tn,),
        in_specs=[pl.BlockSpec((m, d), lambda j: (0, 0)),
                  pl.BlockSpec((d, tn), lambda j: (0, j)),
                  pl.BlockSpec((1, tn), lambda j: (0, j))],
        out_specs=pl.BlockSpec((m, tn), lambda j: (0, j)),
        out_shape=jax.ShapeDtypeStruct((m, n), F32),
        compiler_params=_cparams(1),
    )(c, w_ada, b_ada.reshape(1, n))


def _proj_kernel(x_ref, sh_ref, sc_ref, g_ref, w_ref,
                 q_ref, k_ref, kb_ref, v_ref, vt_ref, gb_ref, u_ref, h_s, gc_s, *, tk):
    j = pl.program_id(1)

    @pl.when(j == 0)
    def _():
        x = x_ref[...]
        r = lax.rsqrt(jnp.mean(x * x, axis=-1, keepdims=True) + EPS)
        h = (x * r) * g_ref[...] * (1.0 + sc_ref[...]) + sh_ref[...]
        h_s[...] = h.astype(BF16)

    p = jnp.dot(h_s[...], w_ref[...], preferred_element_type=F32)

    @pl.when(j == 0)
    def _():
        q_ref[...] = (p * SB_SCALE).astype(BF16)

    @pl.when(j == 1)
    def _():
        k_ref[...] = p
        kb_ref[...] = p.astype(BF16)

    @pl.when(j == 2)
    def _():
        v_ref[...] = p
        for c in range(p.shape[0] // tk):
            vt_ref[c] = p[c * tk:(c + 1) * tk, :].T.astype(BF16)

    @pl.when(j == 3)
    def _():
        gb_ref[...] = p

    @pl.when(j == 4)
    def _():
        gc_s[...] = p

    @pl.when(j == 5)
    def _():
        u_ref[...] = gc_s[...] * p


def _proj_call(x, sh, sc, norm_g, w_in_b, tm, tk):
    n, d = x.shape
    w = ATT_WIDTH
    per_row = sh.shape[0] != 1
    mod_spec = (pl.BlockSpec((tm, d), lambda i, j: (i, 0)) if per_row
                else pl.BlockSpec((1, d), lambda i, j: (0, 0)))
    row = lambda i, j: (i, 0)
    outs = (
        jax.ShapeDtypeStruct((n, w), BF16),
        jax.ShapeDtypeStruct((n, w), F32),
        jax.ShapeDtypeStruct((n, w), BF16),
        jax.ShapeDtypeStruct((n, w), F32),
        jax.ShapeDtypeStruct((n // tk, w, tk), BF16),
        jax.ShapeDtypeStruct((n, w), F32),
        jax.ShapeDtypeStruct((n, w), F32),
    )
    return pl.pallas_call(
        functools.partial(_proj_kernel, tk=tk),
        grid=(n // tm, 6),
        in_specs=[pl.BlockSpec((tm, d), row), mod_spec, mod_spec,
                  pl.BlockSpec((1, d), lambda i, j: (0, 0)),
                  pl.BlockSpec((d, w), lambda i, j: (0, j))],
        out_specs=(pl.BlockSpec((tm, w), row), pl.BlockSpec((tm, w), row), pl.BlockSpec((tm, w), row),
                   pl.BlockSpec((tm, w), row), pl.BlockSpec((tm // tk, w, tk), lambda i, j: (i, 0, 0)),
                   pl.BlockSpec((tm, w), row), pl.BlockSpec((tm, w), row)),
        out_shape=outs,
        scratch_shapes=[pltpu.VMEM((tm, d), BF16), pltpu.VMEM((tm, w), F32)],
        compiler_params=_cparams(2),
    )(x, sh, sc, norm_g, w_in_b)


def _softplus_clamped(z):
    return jnp.maximum(z, jnp.log(1.0 + jnp.exp(jnp.minimum(z, 30.0))))


def _attn_prompt_kernel(bias_ref, q_ref, kb_ref, vt_ref, o_ref, acc_s, *, t, g, nh):
    hg = pl.program_id(0)
    i = pl.program_id(1)
    lane = lax.broadcasted_iota(jnp.int32, (t, LANES), 1)
    q_heads = []
    for h in range(nh):
        q2 = q_ref[:, (h // 2) * LANES:(h // 2 + 1) * LANES]
        keep = (lane < HEAD_DIM) if h % 2 == 0 else (lane >= HEAD_DIM)
        q_heads.append(jnp.where(keep, q2, jnp.zeros_like(q2)))
    biases = [bias_ref[nh * hg + h] for h in range(nh)]
    r_io = lax.broadcasted_iota(jnp.int32, (t, t), 0)
    c_io = lax.broadcasted_iota(jnp.int32, (t, t), 1)
    tri = jnp.where(r_io < c_io, 1.0, 0.0).astype(BF16)
    key_minus_query = r_io - c_io
    acc_s[...] = jnp.zeros_like(acc_s)

    def group(gi, carry, masked):
        run = list(carry)
        tiles = [(b, h) for b in reversed(range(g)) for h in range(nh)]
        st = [dict() for _ in tiles]

        def scores(k):
            b, h = tiles[k]
            blk = gi * g + b
            start = pl.multiple_of(blk * t, t)
            kblk = kb_ref[pl.ds(start, t), (h // 2) * LANES:(h // 2 + 1) * LANES]
            st[k]["z"] = _nt_dot(kblk, q_heads[h]) + biases[h]
            if masked:
                st[k]["vis"] = key_minus_query < (i - blk) * t

        def suffix(k):
            z = st[k]["z"]
            sp = _softplus_clamped(z)
            spm = jnp.where(st[k]["vis"], sp, 0.0) if masked else sp
            st[k]["lb"] = z - sp
            st[k]["row0"] = spm[0:1, :]
            st[k]["suf"] = jnp.dot(tri, spm.astype(BF16), preferred_element_type=F32)

        def values(k):
            b, h = tiles[k]
            suf = st[k]["suf"]
            a = jnp.exp(st[k]["lb"] - suf)
            if masked:
                a = jnp.where(st[k]["vis"], a, 0.0)
            vtb = vt_ref[gi * g + b, h * HEAD_DIM:(h + 1) * HEAD_DIM, :]
            st[k]["o"] = jnp.dot(vtb, a.astype(BF16), preferred_element_type=F32)
            st[k]["tot"] = suf[0:1, :] + st[k]["row0"]

        def accumulate(k):
            b, h = tiles[k]
            rows = slice(h * HEAD_DIM, (h + 1) * HEAD_DIM)
            acc_s[rows, :] += st[k]["o"] * jnp.exp(-run[h])
            run[h] = run[h] + st[k]["tot"]
            st[k].clear()

        stages = ((scores, 0), (suffix, 2), (values, 4), (accumulate, 5))
        for step in range(len(tiles) + stages[-1][1]):
            for fn, lag in stages:
                if 0 <= step - lag < len(tiles):
                    fn(step - lag)
        return tuple(run)

    c0 = jnp.zeros((1, t), F32)
    top = i // g
    carry = group(top, (c0,) * nh, True)
    lax.fori_loop(0, top, lambda s, c: group(top - 1 - s, c, False), carry)
    o_ref[...] = acc_s[...].T


def _attn_prompt_call(bias, qs, kb, vt, t, g, nh):
    n, w = qs.shape
    nk = n // t
    wb = nh * HEAD_DIM
    assert nk % g == 0 and wb % LANES == 0
    return pl.pallas_call(
        functools.partial(_attn_prompt_kernel, t=t, g=g, nh=nh),
        grid=(w // wb, n // t),
        in_specs=[pl.BlockSpec(memory_space=pltpu.SMEM),
                  pl.BlockSpec((t, wb), lambda hg, i: (i, hg)),
                  pl.BlockSpec((n, wb), lambda hg, i: (0, hg), pipeline_mode=pl.Buffered(1)),
                  pl.BlockSpec((nk, wb, t), lambda hg, i: (0, hg, 0), pipeline_mode=pl.Buffered(1))],
        out_specs=pl.BlockSpec((t, wb), lambda hg, i: (i, hg)),
        out_shape=jax.ShapeDtypeStruct((n, w), F32),
        scratch_shapes=[pltpu.VMEM((wb, t), F32)],
        compiler_params=_cparams(2),
    )(bias, qs, kb, vt)


def _attn_sample_kernel(pt_ref, bias_ref, q_ref, kn_ref, vn_ref, *rest, tq, page, pps):
    ck_refs = rest[0:2 * pps:2]
    cv_refs = rest[1:2 * pps:2]
    o_ref, qbd_s, acc_s, carry_s = rest[2 * pps:]
    s = pl.program_id(1)
    n_steps = pl.num_programs(1)
    w = ATT_WIDTH
    nr = tq * ATT_HEADS
    r_io = lax.broadcasted_iota(jnp.int32, (page, page), 0)
    c_io = lax.broadcasted_iota(jnp.int32, (page, page), 1)
    tri = jnp.where(r_io > c_io, 1.0, 0.0).astype(BF16)
    ones = jnp.ones((page, page), BF16)
    head_of_lane = lax.shift_right_logical(lax.broadcasted_iota(jnp.int32, (ATT_HEADS, w), 1), 6)
    head_sel = head_of_lane == lax.broadcasted_iota(jnp.int32, (ATT_HEADS, w), 0)

    def scores(kt):
        return jnp.dot(qbd_s[...], kt.astype(BF16), preferred_element_type=F32) + bias_ref[...]

    def local(z, mask):
        sp = _softplus_clamped(z)
        spm = sp if mask is None else jnp.where(mask, sp, 0.0)
        spb = spm.astype(BF16)
        return (z - sp, jnp.dot(spb, tri, preferred_element_type=F32),
                jnp.dot(spb, ones, preferred_element_type=F32))

    def weights(log_beta, suf, carry, mask):
        a = jnp.exp(log_beta - suf - carry)
        if mask is not None:
            a = jnp.where(mask, a, 0.0)
        return a.astype(BF16)

    @pl.when(s == 0)
    def _():
        q4 = q_ref[0]
        for qi in range(tq):
            blk = jnp.where(head_sel, jnp.broadcast_to(q4[qi:qi + 1, :], (ATT_HEADS, w)), 0.0)
            qbd_s[qi * ATT_HEADS:(qi + 1) * ATT_HEADS, :] = blk.astype(BF16)
        pad = jnp.zeros((page - tq, w), F32)
        kt = jnp.concatenate([kn_ref[0], pad], axis=0).T
        vt = jnp.concatenate([vn_ref[0], pad], axis=0).T
        key_lane = lax.broadcasted_iota(jnp.int32, (nr, page), 1)
        q_of_row = lax.shift_right_logical(lax.broadcasted_iota(jnp.int32, (nr, page), 0), 4)
        mask = key_lane < q_of_row
        log_beta, suf, tot = local(scores(kt), mask)
        a = weights(log_beta, suf, jnp.zeros_like(suf), mask)
        acc_s[...] = _nt_dot(a, vt.astype(BF16))
        carry_s[...] = tot

    @pl.when(s > 0)
    def _():
        z = [None] * pps
        loc = [None] * pps
        out = None
        carry = carry_s[...]
        for step in range(pps + 2):
            if step < pps:
                z[step] = scores(ck_refs[step][0])
            if 0 <= step - 1 < pps:
                loc[step - 1] = local(z[step - 1], None)
            if 0 <= step - 2 < pps:
                lb, suf, tot = loc[step - 2]
                o = _nt_dot(weights(lb, suf, carry, None), cv_refs[step - 2][0].astype(BF16))
                out = o if out is None else out + o
                carry = carry + tot
        acc_s[...] += out
        carry_s[...] = carry

    @pl.when(s == n_steps - 1)
    def _():
        for qi in range(tq):
            blk = acc_s[qi * ATT_HEADS:(qi + 1) * ATT_HEADS, :]
            o_ref[0, qi:qi + 1, :] = jnp.sum(jnp.where(head_sel, blk, 0.0), axis=0, keepdims=True)


def _attn_sample_call(page_table, bias_rows, q, k_new, v_new, cache_kt, cache_vt, pps):
    b, tq, w = q.shape
    n_pages = page_table.shape[1]
    page = cache_kt.shape[2]
    nr = tq * ATT_HEADS
    assert nr % 16 == 0 and page == LANES and n_pages % pps == 0

    def page_idx(j):
        def idx(bi, s, pt):
            return (pt[bi * n_pages + n_pages - 1 - j - pps * (jnp.maximum(s, 1) - 1)], 0, 0)
        return idx

    new_spec = pl.BlockSpec((1, tq, w), lambda bi, s, pt: (bi, 0, 0))
    cache_specs = []
    for j in range(pps):
        cache_specs += [pl.BlockSpec((1, w, page), page_idx(j)), pl.BlockSpec((1, w, page), page_idx(j))]
    grid_spec = pltpu.PrefetchScalarGridSpec(
        num_scalar_prefetch=1,
        grid=(b, n_pages // pps + 1),
        in_specs=[pl.BlockSpec((nr, page), lambda bi, s, pt: (0, 0)), new_spec, new_spec, new_spec] + cache_specs,
        out_specs=new_spec,
        scratch_shapes=[pltpu.VMEM((nr, w), BF16), pltpu.VMEM((nr, w), F32), pltpu.VMEM((nr, page), F32)],
    )
    return pl.pallas_call(
        functools.partial(_attn_sample_kernel, tq=tq, page=page, pps=pps),
        grid_spec=grid_spec,
        out_shape=jax.ShapeDtypeStruct((b, tq, w), F32),
        compiler_params=_cparams(2),
    )(page_table.reshape(-1), bias_rows, q, k_new, v_new, *([cache_kt, cache_vt] * pps))


def _post_kernel(attn_ref, gb_ref, u_ref, p0_ref, p1_ref, x_ref, g1_ref, sc2_ref, sh2_ref,
                 cw_ref, gna_ref, gnc_ref, n2_ref, gm_ref, wo_ref,
                 x1_ref, hhi_ref, hlo_ref, *, group, prev_from_u):
    i = pl.program_id(0)
    u = u_ref[...]
    assert group & (group - 1) == 0
    if prev_from_u:
        first = i == 0
        buf = p0_ref[...]
        prev = p1_ref[...]
        p0 = jnp.where(first, buf[6:7, :], prev[6:7, :])
        p1 = jnp.where(first, buf[7:8, :], prev[7:8, :])
    else:
        p0 = p0_ref[...]
        p1 = p1_ref[...]
    r = lax.broadcasted_iota(jnp.int32, u.shape, 0) & (group - 1)
    um1 = jnp.where(r == 0, p1, pltpu.roll(u, 1, 0))
    um2 = jnp.where(r == 0, p0, jnp.where(r == 1, p1, pltpu.roll(u, 2, 0)))
    cw = cw_ref[...]
    conv = cw[0:1, :] * um2 + cw[1:2, :] * um1 + cw[2:3, :] * u
    yc = gb_ref[...] * conv
    attn = attn_ref[...]
    gm = gm_ref[...]

    def gnorm(v, g):
        msq = jnp.dot((v * v).astype(BF16), gm, preferred_element_type=F32)
        return ((v * lax.rsqrt(msq + EPS)) * g).astype(BF16)

    w = attn.shape[1]
    merged = (jnp.dot(gnorm(attn, gna_ref[...]), wo_ref[0:w, :], preferred_element_type=F32)
              + jnp.dot(gnorm(yc, gnc_ref[...]), wo_ref[w:2 * w, :], preferred_element_type=F32))
    x1 = x_ref[...] + g1_ref[...] * merged
    x1_ref[...] = x1
    r2 = lax.rsqrt(jnp.mean(x1 * x1, axis=-1, keepdims=True) + EPS)
    h2 = (x1 * r2) * n2_ref[...] * (1.0 + sc2_ref[...]) + sh2_ref[...]
    hi, lo = _split_bf16(h2)
    hhi_ref[...] = hi
    hlo_ref[...] = lo


def _post_call(attn, gb, u, prev0, prev1, x, g1, sc2, sh2, conv_w8, gn_attn, gn_conv, norm2, gmat, w_o_b,
               tm, group, prev_from_u):
    n, d = x.shape
    w = attn.shape[1]
    row = lambda i: (i, 0)
    fixed = lambda i: (0, 0)
    per_row = g1.shape[0] != 1
    mod_spec = pl.BlockSpec((tm, d), row) if per_row else pl.BlockSpec((1, d), fixed)
    if prev_from_u:
        sub = tm // 8
        p0_spec = pl.BlockSpec((8, w), fixed)
        p1_spec = pl.BlockSpec((8, w), lambda i: (jnp.maximum(i * sub - 1, 0), 0))
    else:
        p0_spec = pl.BlockSpec((tm, w), row)
        p1_spec = pl.BlockSpec((tm, w), row)
    return pl.pallas_call(
        functools.partial(_post_kernel, group=group, prev_from_u=prev_from_u),
        grid=(n // tm,),
        in_specs=[pl.BlockSpec((tm, w), row), pl.BlockSpec((tm, w), row), pl.BlockSpec((tm, w), row),
                  p0_spec, p1_spec, pl.BlockSpec((tm, d), row), mod_spec, mod_spec, mod_spec,
                  pl.BlockSpec((8, w), fixed), pl.BlockSpec((1, w), fixed), pl.BlockSpec((1, w), fixed),
                  pl.BlockSpec((1, d), fixed), pl.BlockSpec((w, w), fixed), pl.BlockSpec((2 * w, d), fixed)],
        out_specs=(pl.BlockSpec((tm, d), row), pl.BlockSpec((tm, d), row), pl.BlockSpec((tm, d), row)),
        out_shape=(jax.ShapeDtypeStruct((n, d), F32), jax.ShapeDtypeStruct((n, d), BF16),
                   jax.ShapeDtypeStruct((n, d), BF16)),
        compiler_params=_cparams(1),
    )(attn, gb, u, prev0, prev1, x, g1, sc2, sh2, conv_w8, gn_attn, gn_conv, norm2, gmat, w_o_b)


def _extract_topk(work, n_rounds, vals_ref=None):
    rows = lax.broadcasted_iota(jnp.int32, work.shape, 0).astype(F32)
    rank = jnp.full(work.shape, NOT_RANKED, F32)
    big = float(work.shape[0])
    for r in range(n_rounds):
        m = jnp.max(work, axis=0, keepdims=True)
        first = jnp.min(jnp.where(work == m, rows, big), axis=0, keepdims=True)
        sel = rows == first
        rank = jnp.where(sel, float(r + 1), rank)
        work = jnp.where(sel, NEG_INF, work)
        if vals_ref is not None:
            vals_ref[r:r + 1, :] = m
    return rank


def _candidate_pairs():
    return [(a, b) for a in range(PEER_TOPK) for b in range(PEER_TOPK) if (a + 1) * (b + 1) <= PEER_TOPK]


def _route_kernel(hhi_ref, hlo_ref, wqh_ref, wql_ref, skh_ref, skl_ref,
                  rank1_ref, cnt_ref, e0_ref, e1_ref, va_s, vb_s, cand_s, sel_s, *, n_cand_rows):
    hhi = hhi_ref[...]
    q = (jnp.dot(hhi, wqh_ref[...], preferred_element_type=F32)
         + jnp.dot(hhi, wql_ref[...], preferred_element_type=F32)
         + jnp.dot(hlo_ref[...], wqh_ref[...], preferred_element_type=F32))
    scores = []
    for p in range(2):
        qh, ql = _split_bf16(q[:, p * N_KEYS:(p + 1) * N_KEYS])
        skh = skh_ref[0, p]
        scores.append(_nt_dot(skh, qh) + _nt_dot(skh, ql) + _nt_dot(skl_ref[0, p], qh))
    rank0 = _extract_topk(scores[0], PEER_TOPK, va_s)
    rank1 = _extract_topk(scores[1], PEER_TOPK, vb_s)

    pairs = _candidate_pairs()
    cand_s[...] = jnp.full(cand_s.shape, NEG_INF, F32)
    for idx, (a, b) in enumerate(pairs):
        cand_s[idx:idx + 1, :] = va_s[a:a + 1, :] + vb_s[b:b + 1, :]
    sel_s[...] = jnp.where(_extract_topk(cand_s[...], PEER_TOPK) <= float(PEER_TOPK), 1.0, 0.0)

    a_max = va_s[0:1, :]
    b_max = vb_s[0:1, :]
    tm = a_max.shape[1]
    zsum = jnp.zeros((1, tm), F32)
    cnts = [jnp.zeros((1, tm), F32) for _ in range(PEER_TOPK)]
    for idx, (a, b) in enumerate(pairs):
        s_row = sel_s[idx:idx + 1, :]
        cnts[a] = cnts[a] + s_row
        zsum = zsum + s_row * (jnp.exp(va_s[a:a + 1, :] - a_max) * jnp.exp(vb_s[b:b + 1, :] - b_max))
    cnt = jnp.zeros(rank0.shape, F32)
    for a in range(PEER_TOPK):
        cnt = jnp.where(rank0 == float(a + 1), cnts[a], cnt)
    in0 = rank0 <= float(PEER_TOPK)
    in1 = rank1 <= float(PEER_TOPK)
    rank1_ref[0] = rank1
    cnt_ref[0] = cnt
    e0_ref[0] = jnp.where(in0, jnp.exp(jnp.minimum(scores[0] - a_max, 0.0)), 0.0) * (1.0 / zsum)
    e1_ref[0] = jnp.where(in1, jnp.exp(jnp.minimum(scores[1] - b_max, 0.0)), 0.0)


def _route_call(hhi, hlo, wq_hi, wq_lo, sk_hi, sk_lo, tm):
    n, d = hhi.shape
    qd = 2 * N_KEYS
    n_cand_rows = 56
    tab = jax.ShapeDtypeStruct((PEER_HEADS, N_KEYS, n), F32)
    tab_spec = pl.BlockSpec((1, N_KEYS, tm), lambda i, h: (h, 0, i))
    row = lambda i, h: (i, 0)
    return pl.pallas_call(
        functools.partial(_route_kernel, n_cand_rows=n_cand_rows),
        grid=(n // tm, PEER_HEADS),
        in_specs=[pl.BlockSpec((tm, d), row), pl.BlockSpec((tm, d), row),
                  pl.BlockSpec((d, qd), lambda i, h: (0, h)), pl.BlockSpec((d, qd), lambda i, h: (0, h)),
                  pl.BlockSpec((1, 2, N_KEYS, N_KEYS), lambda i, h: (h, 0, 0, 0)),
                  pl.BlockSpec((1, 2, N_KEYS, N_KEYS), lambda i, h: (h, 0, 0, 0))],
        out_specs=(tab_spec, tab_spec, tab_spec, tab_spec),
        out_shape=(tab, tab, tab, tab),
        scratch_shapes=[pltpu.VMEM((PEER_TOPK, tm), F32), pltpu.VMEM((PEER_TOPK, tm), F32),
                        pltpu.VMEM((n_cand_rows, tm), F32), pltpu.VMEM((n_cand_rows, tm), F32)],
        compiler_params=_cparams(2),
    )(hhi, hlo, wq_hi, wq_lo, sk_hi, sk_lo)


def _gelu_tanh(x):
    return 0.5 * x * (1.0 + jnp.tanh(0.7978845608028654 * (x + 0.044715 * (x * x * x))))


def _dense_kernel(hhi_ref, u_ref, v_ref, rank1_ref, cnt_ref, e0_ref, e1_ref, x1_ref, g2_ref, nf_ref,
                  y_ref, acc_s, *, te):
    eb = pl.program_id(1)
    n_eb = pl.num_programs(1)
    tn = hhi_ref.shape[0]
    sub = 2 * N_KEYS
    n_sub = te // sub
    assert te // N_KEYS == 8

    @pl.when(eb == 0)
    def _():
        acc_s[...] = jnp.zeros_like(acc_s)

    hhi = hhi_ref[...]
    acts = [None] * n_sub
    outs = [None] * n_sub

    def score(sb):
        acts[sb] = _nt_dot(u_ref[sb * sub:(sb + 1) * sub, :], hhi)

    def weigh(sb):
        pieces = []
        for i2 in range(sub // N_KEYS):
            ii = sb * (sub // N_KEYS) + i2
            cols = []
            for lc in range(tn // LANES):
                ls = slice(lc * LANES, (lc + 1) * LANES)
                g = jnp.zeros((N_KEYS, LANES), F32)
                for h in range(PEER_HEADS):
                    cnt_row = cnt_ref[h, ii:ii + 1, ls]
                    e0_row = e0_ref[h, ii:ii + 1, ls]
                    g = g + jnp.where(rank1_ref[h, :, ls] <= cnt_row, e1_ref[h, :, ls], 0.0) * e0_row
                cols.append(g * _gelu_tanh(acts[sb][i2 * N_KEYS:(i2 + 1) * N_KEYS, ls]))
            pieces.append(jnp.concatenate(cols, axis=1))
        wgt_t = jnp.concatenate(pieces, axis=0)
        outs[sb] = jnp.dot(wgt_t.T.astype(BF16), v_ref[sb * sub:(sb + 1) * sub, :], preferred_element_type=F32)
        acts[sb] = None

    for step in range(n_sub + 1):
        if step < n_sub:
            score(step)
        if step >= 1:
            weigh(step - 1)
    out = outs[0]
    for o in outs[1:]:
        out = out + o
    acc_s[...] += out

    @pl.when(eb == n_eb - 1)
    def _():
        x = x1_ref[...] + g2_ref[...] * acc_s[...]
        r = lax.rsqrt(jnp.mean(x * x, axis=-1, keepdims=True) + EPS)
        y_ref[...] = (x * r) * nf_ref[...]


def _dense_call(hhi, u_b, v_b, rank1, cnt, e0, e1, x1, g2, norm_f, tn, te):
    n, d = hhi.shape
    n_exp = u_b.shape[0]
    per_row = g2.shape[0] != 1
    row = lambda t, e: (t, 0)
    once = pl.Buffered(1)
    mod_spec = pl.BlockSpec((tn, d), row) if per_row else pl.BlockSpec((1, d), lambda t, e: (0, 0))
    tab_spec = pl.BlockSpec((PEER_HEADS, N_KEYS, tn), lambda t, e: (0, 0, t), pipeline_mode=once)
    grp_spec = pl.BlockSpec((PEER_HEADS, te // N_KEYS, tn), lambda t, e: (0, e, t))
    return pl.pallas_call(
        functools.partial(_dense_kernel, te=te),
        grid=(n // tn, n_exp // te),
        in_specs=[pl.BlockSpec((tn, d), row, pipeline_mode=once),
                  pl.BlockSpec((te, d), lambda t, e: (e, 0)), pl.BlockSpec((te, d), lambda t, e: (e, 0)),
                  tab_spec, grp_spec, grp_spec, tab_spec,
                  pl.BlockSpec((tn, d), row, pipeline_mode=once), mod_spec,
                  pl.BlockSpec((1, d), lambda t, e: (0, 0))],
        out_specs=pl.BlockSpec((tn, d), row),
        out_shape=jax.ShapeDtypeStruct((n, d), F32),
        scratch_shapes=[pltpu.VMEM((tn, d), F32)],
        compiler_params=_cparams(2),
    )(hhi, u_b, v_b, rank1, cnt, e0, e1, x1, g2, norm_f)


def _peer_and_norm(hhi, hlo, x1, g2, weights, tm_route, tn_dense):
    wq_hi, wq_lo, sk_hi, sk_lo, u_b, v_b, norm_f = weights
    rank1, cnt, e0, e1 = _route_call(hhi, hlo, wq_hi, wq_lo, sk_hi, sk_lo, tm_route)
    return _dense_call(hhi, u_b, v_b, rank1, cnt, e0, e1, x1, g2, norm_f, tn_dense, DENSE_EXPERTS)


def kernel(x_prompt, x_sample, cache_k, cache_v, state_conv, page_table, c_prompt, c_sample, w_ada, b_ada,
           norm1, w_in, sb_bias, conv_w, gn_attn, gn_conv, w_o, norm2, w_query, sub_keys, expert_u, expert_v,
           norm_f):
    depth = w_in.shape[0]
    assert depth == 1 and x_prompt.shape[0] == 1
    d = x_prompt.shape[-1]
    seq = x_prompt.shape[1]
    nb, tq = x_sample.shape[:2]
    ns = nb * tq
    w = ATT_WIDTH
    l = 0

    c_all = jnp.concatenate([c_prompt, c_sample], axis=0)
    pad = (-c_all.shape[0]) % 8
    c_all = jnp.pad(c_all, ((0, pad), (0, 0)))
    mod = _mod_call(c_all, w_ada[l], b_ada[l])
    mods_p = [mod[0:1, k * d:(k + 1) * d] for k in range(6)]
    mods_s = [jnp.repeat(mod[1:1 + nb, k * d:(k + 1) * d], tq, axis=0) for k in range(6)]

    w_in_b = w_in[l].astype(BF16)
    w_o_b = w_o[l].astype(BF16)
    wq = w_query[l]
    wq_hi = wq.astype(BF16)
    wq_lo = (wq - wq_hi.astype(F32)).astype(BF16)
    sk = sub_keys[l]
    sk_hi = sk.astype(BF16)
    sk_lo = (sk - sk_hi.astype(F32)).astype(BF16)
    peer_w = (wq_hi, wq_lo, sk_hi, sk_lo, expert_u[l].astype(BF16), expert_v[l].astype(BF16),
              norm_f.reshape(1, d))
    grp = jnp.arange(w, dtype=jnp.int32) // HEAD_DIM
    gmat = jnp.where(grp[:, None] == grp[None, :], 1.0 / HEAD_DIM, 0.0).astype(BF16)
    conv_w8 = jnp.pad(conv_w[l], ((0, 8 - conv_w.shape[1]), (0, 0)))
    n1 = norm1[l].reshape(1, d)
    n2 = norm2[l].reshape(1, d)
    gna = gn_attn[l].reshape(1, w)
    gnc = gn_conv[l].reshape(1, w)
    bias = sb_bias[l].astype(F32)

    xp = x_prompt.reshape(seq, d)
    qs, k_p, kb, v_p, vt, gb, u = _proj_call(xp, mods_p[0], mods_p[1], n1, w_in_b, PROJ_ROWS, ATT_TILE)
    attn = _attn_prompt_call(bias, qs, kb, vt, ATT_TILE, ATT_GROUP, ATT_HEADS_PER_STEP)
    zero_buf = jnp.zeros((8, w), F32)
    x1, hhi, hlo = _post_call(attn, gb, u, zero_buf, u, xp, mods_p[2], mods_p[4], mods_p[3], conv_w8,
                              gna, gnc, n2, gmat, w_o_b, POST_ROWS, POST_ROWS, True)
    y_p = _peer_and_norm(hhi, hlo, x1, mods_p[5], peer_w, ROUTE_ROWS, DENSE_ROWS)

    xs = x_sample.reshape(ns, d)
    qs_s, k_s, _, v_s, _, gb_s, u_s = _proj_call(xs, mods_s[0], mods_s[1], n1, w_in_b, ns, ns)
    ckt = jnp.transpose(cache_k[l], (0, 2, 3, 1)).reshape(cache_k.shape[1], w, cache_k.shape[2])
    cvt = jnp.transpose(cache_v[l], (0, 2, 3, 1)).reshape(cache_v.shape[1], w, cache_v.shape[2])
    bias_rows = jnp.broadcast_to(jnp.tile(bias, tq)[:, None], (tq * ATT_HEADS, LANES))
    q_s = qs_s.astype(F32)
    attn_s = _attn_sample_call(page_table, bias_rows, q_s.reshape(nb, tq, w), k_s.reshape(nb, tq, w),
                               v_s.reshape(nb, tq, w), ckt, cvt, PAGES_PER_STEP).reshape(ns, w)
    st = state_conv[l]
    prev0 = jnp.repeat(st[:, 0, :], tq, axis=0)
    prev1 = jnp.repeat(st[:, 1, :], tq, axis=0)
    x1_s, hhi_s, hlo_s = _post_call(attn_s, gb_s, u_s, prev0, prev1, xs, mods_s[2], mods_s[4], mods_s[3],
                                    conv_w8, gna, gnc, n2, gmat, w_o_b, ns, tq, False)
    y_s = _peer_and_norm(hhi_s, hlo_s, x1_s, mods_s[5], peer_w, ns, ns)

    heads = (ATT_HEADS, HEAD_DIM)
    return (y_p.reshape(1, seq, d),
            y_s.reshape(nb, tq, d),
            k_p.reshape(1, 1, seq, *heads),
            v_p.reshape(1, 1, seq, *heads),
            u[seq - 2:seq].reshape(1, 1, 2, w),
            k_s.reshape(1, nb, tq, *heads),
            v_s.reshape(1, nb, tq, *heads),
            u_s.reshape(nb, tq, w)[:, tq - 2:tq].reshape(1, nb, 2, w))
```
